```python
import jax, jax.numpy as jnp
from jax import lax
import numpy as np

D_MODEL = 1024
BATCH = 8
SEQ = 2048
DEPTH = 4
DEC_BATCH = 32
DEC_SEQ = 4
PAST_LEN = 16384
PAGE_SIZE = 128

N_MIXERS = 3
N_A_LAYERS = (DEPTH + 2) // 3
N_B_LAYERS = (DEPTH + 1) // 3
N_C_LAYERS = DEPTH // 3

D_FF = (-(-8 * D_MODEL // (3 * 256))) * 256

A_CHUNK = 128
A_WIDTH = D_MODEL
A_GROUPS = 8
A_GROUP_DIM = A_WIDTH // A_GROUPS
LN_EPS = 1e-5

MLA_HEADS = 8
MLA_Q_LORA = 384
MLA_KV_LORA = 256
MLA_NOPE = 128
MLA_ROPE = 64
MLA_V = 128
MLA_LAT = MLA_KV_LORA + MLA_ROPE
MLA_SCALE = (MLA_NOPE + MLA_ROPE) ** -0.5
ROPE_THETA = 10000.0
Q_BLOCK = 128

LRU_WIDTH = 1280
LRU_BLOCKS = 10
LRU_BLOCK_DIM = LRU_WIDTH // LRU_BLOCKS
CONV_WIDTH = 4
LRU_C = 8.0

RMS_EPS = 1e-6

kernel_name = 'hybrid_gmlp_mla_rglru_step'


def rmsnorm(x, g):
    xf = x.astype(jnp.float32)
    y = xf * lax.rsqrt(jnp.mean(xf * xf, axis=-1, keepdims=True) + RMS_EPS)
    return (y * g.astype(jnp.float32)).astype(x.dtype)


def layernorm(x, g, b):
    xf = x.astype(jnp.float32)
    mu = jnp.mean(xf, axis=-1, keepdims=True)
    xc = xf - mu
    y = xc * lax.rsqrt(jnp.mean(xc * xc, axis=-1, keepdims=True) + LN_EPS)
    return (y * g.astype(jnp.float32) + b.astype(jnp.float32)).astype(x.dtype)


def swiglu(x, w_gate, w_up, w_down):
    return (jax.nn.silu(x @ w_gate) * (x @ w_up)) @ w_down


def chunk_mlp(x, w_in, ln_g, ln_b, w_s, b_s, w_out):
    bsz, seqlen, _ = x.shape
    z = jax.nn.gelu(x @ w_in)
    u, v = jnp.split(z, 2, axis=-1)
    v = layernorm(v, ln_g, ln_b)
    n_chunks = -(-seqlen // A_CHUNK)
    pad = n_chunks * A_CHUNK - seqlen
    vp = jnp.pad(v, ((0, 0), (0, pad), (0, 0))).reshape(bsz, n_chunks, A_CHUNK, A_GROUPS, A_GROUP_DIM)
    mask = jnp.tril(jnp.ones((A_CHUNK, A_CHUNK), dtype=bool))
    w = jnp.where(mask[None], w_s, 0)
    mixed = jnp.einsum('gts,bcsgd->bctgd', w, vp) + b_s.T[None, None, :, :, None]
    mixed = mixed.reshape(bsz, n_chunks * A_CHUNK, A_WIDTH)[:, :seqlen]
    return (u * mixed) @ w_out, v


def rope_angles(pos):
    inv = 1.0 / (ROPE_THETA ** (jnp.arange(0, MLA_ROPE, 2, dtype=jnp.float32) / MLA_ROPE))
    ang = pos.astype(jnp.float32)[:, None] * inv[None, :]
    return jnp.cos(ang), jnp.sin(ang)


def apply_rope(x, cos, sin):
    xf = x.astype(jnp.float32)
    x1, x2 = jnp.split(xf, 2, axis=-1)
    return jnp.concatenate([x1 * cos - x2 * sin, x1 * sin + x2 * cos], axis=-1).astype(x.dtype)


def mla_project(x, cos, sin, w_dq, q_norm, w_uq, w_dkv, kv_norm, w_uk):
    bsz, seqlen, _ = x.shape
    cq = rmsnorm(x @ w_dq, q_norm)
    q = (cq @ w_uq).reshape(bsz, seqlen, MLA_HEADS, MLA_NOPE + MLA_ROPE)
    q_nope, q_pe = q[..., :MLA_NOPE], q[..., MLA_NOPE:]
    kv = x @ w_dkv
    c_kv = rmsnorm(kv[..., :MLA_KV_LORA], kv_norm)
    k_pe = apply_rope(kv[..., MLA_KV_LORA:], cos, sin)
    q_pe = apply_rope(q_pe, cos[:, None, :], sin[:, None, :])
    q_lat = jnp.einsum('blhn,chn->blhc', q_nope, w_uk)
    q_full = jnp.concatenate([q_lat, q_pe], axis=-1)
    kv_row = jnp.concatenate([c_kv, k_pe], axis=-1)
    return q_full, kv_row


def mla_attend(q, q_pos, keys, k_pos):
    s = jnp.einsum('bqhc,bkc->bhqk', q, keys, preferred_element_type=jnp.float32) * MLA_SCALE
    mask = k_pos[None, :] <= q_pos[:, None]
    s = jnp.where(mask, s, -jnp.inf)
    p = jax.nn.softmax(s, axis=-1).astype(keys.dtype)
    return jnp.einsum('bhqk,bkc->bqhc', p, keys[..., :MLA_KV_LORA])


def mla_prompt_attention(q, kv, pos):
    bsz, seqlen, _, _ = q.shape
    nb = seqlen // Q_BLOCK
    qb = q.reshape(bsz, nb, Q_BLOCK, MLA_HEADS, MLA_LAT).transpose(1, 0, 2, 3, 4)
    qpos = pos.reshape(nb, Q_BLOCK)
    o = lax.map(lambda a: mla_attend(a[0], a[1], kv, pos), (qb, qpos))
    return o.transpose(1, 0, 2, 3, 4).reshape(bsz, seqlen, MLA_HEADS, MLA_KV_LORA)


def mla_output(o, w_uv, w_out):
    bsz, seqlen = o.shape[0], o.shape[1]
    ov = jnp.einsum('blhc,chv->blhv', o, w_uv).reshape(bsz, seqlen, MLA_HEADS * MLA_V)
    return ov @ w_out


def causal_conv(x, buf, w, b):
    seqlen = x.shape[1]
    xx = jnp.concatenate([buf.astype(x.dtype), x], axis=1)
    out = b + sum(w[k] * xx[:, k:k + seqlen] for k in range(CONV_WIDTH))
    return out, xx[:, -(CONV_WIDTH - 1):]


def block_diag(x, w, b):
    bsz, seqlen, _ = x.shape
    xb = x.reshape(bsz, seqlen, LRU_BLOCKS, LRU_BLOCK_DIM)
    return jnp.einsum('blni,nij->blnj', xb, w).reshape(bsz, seqlen, LRU_WIDTH) + b


def rglru(x, h0, w_a, b_a, w_i, b_i, lam):
    r = jax.nn.sigmoid(block_diag(x, w_a, b_a).astype(jnp.float32))
    gi = jax.nn.sigmoid(block_diag(x, w_i, b_i).astype(jnp.float32))
    log_a = -LRU_C * r * jax.nn.softplus(-lam.astype(jnp.float32))
    a = jnp.exp(log_a)
    bterm = jnp.sqrt(-jnp.expm1(2.0 * log_a)) * gi * x.astype(jnp.float32)
    bterm = bterm.at[:, 0].add(a[:, 0] * h0.astype(jnp.float32))

    def combine(left, right):
        a1, b1 = left
        a2, b2 = right
        return a1 * a2, a2 * b1 + b2

    _, h = lax.associative_scan(combine, (a, bterm), axis=1)
    return h.astype(x.dtype), h[:, -1].astype(x.dtype)


def recurrent_block(x, h0, conv_buf, w_x, w_gate, conv_w, conv_b, w_a, b_a, w_i, b_i, lam, w_out):
    gate = jax.nn.gelu(x @ w_gate)
    xc, new_buf = causal_conv(x @ w_x, conv_buf, conv_w, conv_b)
    h, h_last = rglru(xc, h0, w_a, b_a, w_i, b_i, lam)
    return (gate * h) @ w_out, h_last, new_buf


def setup_inputs(seed: int = 0) -> dict:
    key = jax.random.key(seed)
    ks = iter(jax.random.split(key, 64))

    def nrm(shape, fan_in=1):
        return jax.random.normal(next(ks), shape, jnp.float32) * (fan_in ** -0.5)

    def gain(shape):
        return 1.0 + 0.02 * jax.random.normal(next(ks), shape, jnp.float32)

    def small(shape):
        return 0.01 * jax.random.normal(next(ks), shape, jnp.float32)

    n_pages = PAST_LEN // PAGE_SIZE
    n_used = DEC_BATCH * n_pages
    n_pool = n_used + n_used // 4

    x_prompt = nrm((BATCH, SEQ, D_MODEL))
    x_sample = nrm((DEC_BATCH, DEC_SEQ, D_MODEL))
    cache_mla = nrm((N_B_LAYERS, n_pool, PAGE_SIZE, MLA_LAT))
    state_lru_h = 0.5 * nrm((N_C_LAYERS, DEC_BATCH, LRU_WIDTH))
    state_lru_conv = nrm((N_C_LAYERS, DEC_BATCH, CONV_WIDTH - 1, LRU_WIDTH))
    page_table = jax.random.permutation(next(ks), n_pool)[:n_used].reshape(DEC_BATCH, n_pages).astype(jnp.int32)

    a0 = jax.random.uniform(next(ks), (N_C_LAYERS, LRU_WIDTH), jnp.float32, 0.9, 0.999)
    a_base = a0 ** (1.0 / LRU_C)
    c_lambda = jnp.log(a_base) - jnp.log1p(-a_base)

    return {
        'x_prompt': x_prompt,
        'x_sample': x_sample,
        'cache_mla': cache_mla,
        'state_lru_h': state_lru_h,
        'state_lru_conv': state_lru_conv,
        'page_table': page_table,
        'norm_mix': gain((DEPTH, D_MODEL)),
        'norm_ffn': gain((DEPTH, D_MODEL)),
        'norm_out': gain((D_MODEL,)),
        'a_w_in': nrm((N_A_LAYERS, D_MODEL, 2 * A_WIDTH), D_MODEL),
        'a_ln_g': gain((N_A_LAYERS, A_WIDTH)),
        'a_ln_b': small((N_A_LAYERS, A_WIDTH)),
        'a_w_s': nrm((N_A_LAYERS, A_GROUPS, A_CHUNK, A_CHUNK), A_CHUNK),
        'a_b_s': gain((N_A_LAYERS, A_GROUPS, A_CHUNK)),
        'a_w_out': nrm((N_A_LAYERS, A_WIDTH, D_MODEL), A_WIDTH),
        'b_w_dq': nrm((N_B_LAYERS, D_MODEL, MLA_Q_LORA), D_MODEL),
        'b_q_norm': gain((N_B_LAYERS, MLA_Q_LORA)),
        'b_w_uq': nrm((N_B_LAYERS, MLA_Q_LORA, MLA_HEADS * (MLA_NOPE + MLA_ROPE)), MLA_Q_LORA),
        'b_w_dkv': nrm((N_B_LAYERS, D_MODEL, MLA_LAT), D_MODEL),
        'b_kv_norm': gain((N_B_LAYERS, MLA_KV_LORA)),
        'b_w_uk': nrm((N_B_LAYERS, MLA_KV_LORA, MLA_HEADS, MLA_NOPE), MLA_KV_LORA),
        'b_w_uv': nrm((N_B_LAYERS, MLA_KV_LORA, MLA_HEADS, MLA_V), MLA_KV_LORA),
        'b_w_out': nrm((N_B_LAYERS, MLA_HEADS * MLA_V, D_MODEL), MLA_HEADS * MLA_V),
        'c_w_x': nrm((N_C_LAYERS, D_MODEL, LRU_WIDTH), D_MODEL),
        'c_w_gate': nrm((N_C_LAYERS, D_MODEL, LRU_WIDTH), D_MODEL),
        'c_conv_w': nrm((N_C_LAYERS, CONV_WIDTH, LRU_WIDTH), CONV_WIDTH),
        'c_conv_b': small((N_C_LAYERS, LRU_WIDTH)),
        'c_w_a': nrm((N_C_LAYERS, LRU_BLOCKS, LRU_BLOCK_DIM, LRU_BLOCK_DIM), LRU_BLOCK_DIM),
        'c_b_a': small((N_C_LAYERS, LRU_WIDTH)),
        'c_w_i': nrm((N_C_LAYERS, LRU_BLOCKS, LRU_BLOCK_DIM, LRU_BLOCK_DIM), LRU_BLOCK_DIM),
        'c_b_i': small((N_C_LAYERS, LRU_WIDTH)),
        'c_lambda': c_lambda,
        'c_w_out': nrm((N_C_LAYERS, LRU_WIDTH, D_MODEL), LRU_WIDTH),
        'f_w_gate': nrm((DEPTH, D_MODEL, D_FF), D_MODEL),
        'f_w_up': nrm((DEPTH, D_MODEL, D_FF), D_MODEL),
        'f_w_down': nrm((DEPTH, D_FF, D_MODEL), D_FF),
    }


def reference(x_prompt, x_sample, cache_mla, state_lru_h, state_lru_conv, page_table,
              norm_mix, norm_ffn, norm_out,
              a_w_in, a_ln_g, a_ln_b, a_w_s, a_b_s, a_w_out,
              b_w_dq, b_q_norm, b_w_uq, b_w_dkv, b_kv_norm, b_w_uk, b_w_uv, b_w_out,
              c_w_x, c_w_gate, c_conv_w, c_conv_b, c_w_a, c_b_a, c_w_i, c_b_i, c_lambda, c_w_out,
              f_w_gate, f_w_up, f_w_down):
    past_len = page_table.shape[1] * PAGE_SIZE
    n_pr, len_p = x_prompt.shape[0], x_prompt.shape[1]
    n_dec, len_s = x_sample.shape[0], x_sample.shape[1]
    pos_p = jnp.arange(len_p, dtype=jnp.int32)
    pos_s = past_len + jnp.arange(len_s, dtype=jnp.int32)
    k_pos_s = jnp.arange(past_len + len_s, dtype=jnp.int32)
    cos_p, sin_p = rope_angles(pos_p)
    cos_s, sin_s = rope_angles(pos_s)

    hp, hs = x_prompt, x_sample
    chunk_v_s, mla_p, mla_s = [], [], []
    lru_h_p, lru_h_s, lru_c_p, lru_c_s = [], [], [], []
    for layer in range(DEPTH):
        kind, j = layer % N_MIXERS, layer // N_MIXERS
        zp = rmsnorm(hp, norm_mix[layer])
        zs = rmsnorm(hs, norm_mix[layer])
        if kind == 0:
            a_args = (a_w_in[j], a_ln_g[j], a_ln_b[j], a_w_s[j], a_b_s[j], a_w_out[j])
            yp, _ = chunk_mlp(zp, *a_args)
            ys, v_s = chunk_mlp(zs, *a_args)
            chunk_v_s.append(v_s)
        elif kind == 1:
            b_args = (b_w_dq[j], b_q_norm[j], b_w_uq[j], b_w_dkv[j], b_kv_norm[j], b_w_uk[j])
            q_p, kv_p = mla_project(zp, cos_p, sin_p, *b_args)
            o_p = mla_prompt_attention(q_p, kv_p, pos_p)
            q_s, kv_s = mla_project(zs, cos_s, sin_s, *b_args)
            past = cache_mla[j][page_table].reshape(n_dec, past_len, MLA_LAT)
            keys = jnp.concatenate([past, kv_s.astype(past.dtype)], axis=1)
            o_s = mla_attend(q_s, pos_s, keys, k_pos_s)
            yp = mla_output(o_p, b_w_uv[j], b_w_out[j])
            ys = mla_output(o_s, b_w_uv[j], b_w_out[j])
            mla_p.append(kv_p)
            mla_s.append(kv_s)
        else:
            c_args = (c_w_x[j], c_w_gate[j], c_conv_w[j], c_conv_b[j], c_w_a[j], c_b_a[j],
                      c_w_i[j], c_b_i[j], c_lambda[j], c_w_out[j])
            h0_p = jnp.zeros((n_pr, LRU_WIDTH), x_prompt.dtype)
            buf0_p = jnp.zeros((n_pr, CONV_WIDTH - 1, LRU_WIDTH), x_prompt.dtype)
            yp, hl_p, cb_p = recurrent_block(zp, h0_p, buf0_p, *c_args)
            ys, hl_s, cb_s = recurrent_block(zs, state_lru_h[j], state_lru_conv[j], *c_args)
            lru_h_p.append(hl_p)
            lru_h_s.append(hl_s)
            lru_c_p.append(cb_p)
            lru_c_s.append(cb_s)
        hp = hp + yp
        hs = hs + ys
        hp = hp + swiglu(rmsnorm(hp, norm_ffn[layer]), f_w_gate[layer], f_w_up[layer], f_w_down[layer])
        hs = hs + swiglu(rmsnorm(hs, norm_ffn[layer]), f_w_gate[layer], f_w_up[layer], f_w_down[layer])

    y_prompt = rmsnorm(hp, norm_out)
    y_sample = rmsnorm(hs, norm_out)
    return (y_prompt, y_sample, jnp.stack(chunk_v_s), jnp.stack(mla_p), jnp.stack(mla_s),
            jnp.stack(lru_h_p), jnp.stack(lru_h_s), jnp.stack(lru_c_p), jnp.stack(lru_c_s))
```

```python
import functools

import jax
import jax.numpy as jnp
from jax import lax
from jax.experimental import pallas as pl
from jax.experimental.pallas import tpu as pltpu

BF = jnp.bfloat16
F32 = jnp.float32

D_MODEL = 1024
D_FF = 2816
A_CHUNK = 128
A_GROUPS = 8
MLA_HEADS = 8
MLA_Q_LORA = 384
MLA_KV_LORA = 256
MLA_NOPE = 128
MLA_ROPE = 64
MLA_V = 128
MLA_LAT = MLA_KV_LORA + MLA_ROPE
MLA_SCALE = (MLA_NOPE + MLA_ROPE) ** -0.5
ROPE_THETA = 10000.0
PAGE_SIZE = 128
LRU_WIDTH = 1280
LRU_BLOCKS = 10
LRU_BLOCK_DIM = 128
CONV_WIDTH = 4
LRU_C = 8.0
RMS_EPS = 1e-6
LN_EPS = 1e-5

SUBLANES = 8
VMEM_LIMIT = 56 * 1024 * 1024


def _params(*sem):
    return pltpu.CompilerParams(dimension_semantics=sem, vmem_limit_bytes=VMEM_LIMIT)


def _dot(a, b):
    return jnp.dot(a.astype(BF), b.astype(BF), preferred_element_type=F32)


def _dot_nt(a, b):
    return lax.dot_general(a.astype(BF), b.astype(BF), (((1,), (1,)), ((), ())),
                           preferred_element_type=F32)


def _rmsnorm(x, g):
    return x * lax.rsqrt(jnp.mean(x * x, axis=-1, keepdims=True) + RMS_EPS) * g


def _layernorm(x, g, b):
    mu = jnp.mean(x, axis=-1, keepdims=True)
    xc = x - mu
    return xc * lax.rsqrt(jnp.mean(xc * xc, axis=-1, keepdims=True) + LN_EPS) * g + b


def _full(shape):
    return pl.BlockSpec(shape, lambda *_: (0,) * len(shape))


def _ffn_kernel(x_ref, g_ref, wg_ref, wu_ref, wd_ref, gout_ref, o_ref, xn_ref, acc_ref, *,
                final_norm):
    j = pl.program_id(1)

    @pl.when(j == 0)
    def _():
        xn_ref[...] = _rmsnorm(x_ref[...], g_ref[...]).astype(BF)
        acc_ref[...] = jnp.zeros_like(acc_ref)

    xn = xn_ref[...]
    gate = _dot(xn, wg_ref[...])
    up = _dot(xn, wu_ref[...])
    acc_ref[...] += _dot(gate * jax.nn.sigmoid(gate) * up, wd_ref[...])

    @pl.when(j == pl.num_programs(1) - 1)
    def _():
        y = x_ref[...] + acc_ref[...]
        if final_norm:
            y = _rmsnorm(y, gout_ref[...])
        o_ref[...] = y


def _ffn(h, g, wg, wu, wd, gout, *, tm, tf, final_norm):
    t = h.shape[0]
    return pl.pallas_call(
        functools.partial(_ffn_kernel, final_norm=final_norm),
        grid=(t // tm, D_FF // tf),
        in_specs=[
            pl.BlockSpec((tm, D_MODEL), lambda i, j: (i, 0)),
            _full((1, D_MODEL)),
            pl.BlockSpec((D_MODEL, tf), lambda i, j: (0, j)),
            pl.BlockSpec((D_MODEL, tf), lambda i, j: (0, j)),
            pl.BlockSpec((tf, D_MODEL), lambda i, j: (j, 0)),
            _full((1, D_MODEL)),
        ],
        out_specs=pl.BlockSpec((tm, D_MODEL), lambda i, j: (i, 0)),
        out_shape=jax.ShapeDtypeStruct((t, D_MODEL), F32),
        scratch_shapes=[pltpu.VMEM((tm, D_MODEL), BF), pltpu.VMEM((tm, D_MODEL), F32)],
        compiler_params=_params("parallel", "arbitrary"),
        name="ffn",
    )(h, g, wg, wu, wd, gout)


def _mixer_a_kernel(x_ref, g_ref, win_ref, lng_ref, lnb_ref, wmix_ref, bmix_ref, wout_ref,
                    o_ref, v_ref, u_scr, vb_scr, uv_scr):
    tm = x_ref.shape[0]
    x = x_ref[...]
    xn = _rmsnorm(x, g_ref[...]).astype(BF)
    u_scr[...] = jax.nn.gelu(_dot(xn, win_ref[:, :D_MODEL]))
    v = _layernorm(jax.nn.gelu(_dot(xn, win_ref[:, D_MODEL:])), lng_ref[...], lnb_ref[...])
    v_ref[...] = v
    vb_scr[...] = v.astype(BF)
    for c in range(tm // A_CHUNK):
        rows = slice(c * A_CHUNK, (c + 1) * A_CHUNK)
        for g in range(A_GROUPS):
            cols = slice(g * A_CHUNK, (g + 1) * A_CHUNK)
            mixed = _dot(wmix_ref[g], vb_scr[rows, cols]) + bmix_ref[:, g:g + 1]
            uv_scr[rows, cols] = (u_scr[rows, cols] * mixed).astype(BF)
    o_ref[...] = x + _dot(uv_scr[...], wout_ref[...])


def _mixer_a(h, g, win, lng, lnb, wmix, bmix, wout, *, tm):
    t = h.shape[0]
    return pl.pallas_call(
        _mixer_a_kernel,
        grid=(t // tm,),
        in_specs=[
            pl.BlockSpec((tm, D_MODEL), lambda i: (i, 0)),
            _full((1, D_MODEL)),
            _full((D_MODEL, 2 * D_MODEL)),
            _full((1, D_MODEL)),
            _full((1, D_MODEL)),
            _full((A_GROUPS, A_CHUNK, A_CHUNK)),
            _full((A_CHUNK, A_GROUPS)),
            _full((D_MODEL, D_MODEL)),
        ],
        out_specs=[pl.BlockSpec((tm, D_MODEL), lambda i: (i, 0)),
                   pl.BlockSpec((tm, D_MODEL), lambda i: (i, 0))],
        out_shape=[jax.ShapeDtypeStruct((t, D_MODEL), F32),
                   jax.ShapeDtypeStruct((t, D_MODEL), F32)],
        scratch_shapes=[pltpu.VMEM((tm, D_MODEL), F32), pltpu.VMEM((tm, D_MODEL), BF),
                        pltpu.VMEM((tm, D_MODEL), BF)],
        compiler_params=_params("parallel"),
        name="mixer_a",
    )(h, g, win, lng, lnb, wmix, bmix, wout)


def _mla_proj_kernel(x_ref, g_ref, cos_ref, sin_ref, wdq_ref, qn_ref, wuqn_ref, wuqa_ref,
                     wuqb_ref, wdkv_ref, wdkvb_ref, kvn_ref, wuk_ref,
                     kv_ref, qlat_ref, qpe_ref):
    tq = x_ref.shape[0]
    xn = _rmsnorm(x_ref[...], g_ref[...]).astype(BF)
    cq = _rmsnorm(_dot(xn, wdq_ref[...]), qn_ref[...]).astype(BF)
    q_nope = _dot(cq, wuqn_ref[...])
    cos = cos_ref[...]
    sin = sin_ref[...]
    q_pe = _dot(cq, wuqa_ref[...]) * cos + _dot(cq, wuqb_ref[...]) * sin
    kv = _dot(xn, wdkv_ref[...])
    kv_ref[:, :MLA_KV_LORA] = _rmsnorm(kv[:, :MLA_KV_LORA], kvn_ref[...])
    kv_ref[:, MLA_KV_LORA:] = (kv[:, MLA_KV_LORA:] * cos[:, :MLA_ROPE]
                               + _dot(xn, wdkvb_ref[...]) * sin[:, :MLA_ROPE])
    for h in range(MLA_HEADS):
        rows = slice(h * tq, (h + 1) * tq)
        qlat_ref[0, rows, :] = _dot(q_nope[:, h * MLA_NOPE:(h + 1) * MLA_NOPE],
                                    wuk_ref[h]).astype(BF)
        qpe_ref[0, rows, :] = q_pe[:, h * MLA_ROPE:(h + 1) * MLA_ROPE].astype(BF)


def _mla_proj(h, g, cos, sin, w, *, tq):
    t = h.shape[0]
    nb = t // tq
    npos = cos.shape[0] // tq
    hq = MLA_HEADS * tq
    return pl.pallas_call(
        _mla_proj_kernel,
        grid=(nb,),
        in_specs=[
            pl.BlockSpec((tq, D_MODEL), lambda i: (i, 0)),
            _full((1, D_MODEL)),
            pl.BlockSpec((tq, MLA_HEADS * MLA_ROPE), lambda i: (i % npos, 0)),
            pl.BlockSpec((tq, MLA_HEADS * MLA_ROPE), lambda i: (i % npos, 0)),
            _full((D_MODEL, MLA_Q_LORA)),
            _full((1, MLA_Q_LORA)),
            _full((MLA_Q_LORA, MLA_HEADS * MLA_NOPE)),
            _full((MLA_Q_LORA, MLA_HEADS * MLA_ROPE)),
            _full((MLA_Q_LORA, MLA_HEADS * MLA_ROPE)),
            _full((D_MODEL, MLA_LAT)),
            _full((D_MODEL, MLA_ROPE)),
            _full((1, MLA_KV_LORA)),
            _full((MLA_HEADS, MLA_NOPE, MLA_KV_LORA)),
        ],
        out_specs=[pl.BlockSpec((tq, MLA_LAT), lambda i: (i, 0)),
                   pl.BlockSpec((1, hq, MLA_KV_LORA), lambda i: (i, 0, 0)),
                   pl.BlockSpec((1, hq, MLA_ROPE), lambda i: (i, 0, 0))],
        out_shape=[jax.ShapeDtypeStruct((t, MLA_LAT), F32),
                   jax.ShapeDtypeStruct((nb, hq, MLA_KV_LORA), BF),
                   jax.ShapeDtypeStruct((nb, hq, MLA_ROPE), BF)],
        compiler_params=_params("parallel"),
        name="mla_proj",
    )(h, g, cos, sin, w["wdq"], w["qn"], w["wuqn"], w["wuqa"], w["wuqb"], w["wdkv"],
      w["wdkvb"], w["kvn"], w["wuk"])


def _softmax_step(s, kb, m_scr, l_scr, acc_scr):
    m_prev = m_scr[...]
    m_new = jnp.maximum(m_prev, jnp.max(s, axis=-1, keepdims=True))
    alpha = jnp.exp(m_prev - m_new)
    p = jnp.exp(s - m_new)
    l_scr[...] = alpha * l_scr[...] + jnp.sum(p, axis=-1, keepdims=True)
    acc_scr[...] = alpha * acc_scr[...] + _dot(p, kb[:, :MLA_KV_LORA])
    m_scr[...] = m_new


def _scores(qlat, qpe, kb):
    return (_dot_nt(qlat, kb[:, :MLA_KV_LORA]) + _dot_nt(qpe, kb[:, MLA_KV_LORA:])) * MLA_SCALE


def _mla_prompt_attn_kernel(qlat_ref, qpe_ref, kv_ref, o_ref, m_scr, l_scr, acc_scr, *, tq, tk):
    i = pl.program_id(1)
    j = pl.program_id(2)

    @pl.when(j == 0)
    def _():
        m_scr[...] = jnp.full_like(m_scr, -jnp.inf)
        l_scr[...] = jnp.zeros_like(l_scr)
        acc_scr[...] = jnp.zeros_like(acc_scr)

    @pl.when(j * tk < (i + 1) * tq)
    def _():
        kb = kv_ref[...].astype(BF)
        s = _scores(qlat_ref[0], qpe_ref[0], kb)
        q_pos = i * tq + (lax.broadcasted_iota(jnp.int32, s.shape, 0) & (tq - 1))
        k_pos = j * tk + lax.broadcasted_iota(jnp.int32, s.shape, 1)
        _softmax_step(jnp.where(k_pos <= q_pos, s, -jnp.inf), kb, m_scr, l_scr, acc_scr)

    @pl.when(j == pl.num_programs(2) - 1)
    def _():
        o_ref[0] = (acc_scr[...] / l_scr[...]).astype(o_ref.dtype)


def _mla_prompt_attn(qlat, qpe, kv, *, n_batch, seq, tq, tk):
    nq, nk = seq // tq, seq // tk
    hq = MLA_HEADS * tq

    def kv_map(b, i, j):
        return (b * nk + jnp.minimum(j, ((i + 1) * tq - 1) // tk), 0)

    return pl.pallas_call(
        functools.partial(_mla_prompt_attn_kernel, tq=tq, tk=tk),
        grid=(n_batch, nq, nk),
        in_specs=[
            pl.BlockSpec((1, hq, MLA_KV_LORA), lambda b, i, j: (b * nq + i, 0, 0)),
            pl.BlockSpec((1, hq, MLA_ROPE), lambda b, i, j: (b * nq + i, 0, 0)),
            pl.BlockSpec((tk, MLA_LAT), kv_map),
        ],
        out_specs=pl.BlockSpec((1, hq, MLA_KV_LORA), lambda b, i, j: (b * nq + i, 0, 0)),
        out_shape=jax.ShapeDtypeStruct((n_batch * nq, hq, MLA_KV_LORA), BF),
        scratch_shapes=[pltpu.VMEM((hq, 1), F32), pltpu.VMEM((hq, 1), F32),
                        pltpu.VMEM((hq, MLA_KV_LORA), F32)],
        compiler_params=_params("parallel", "parallel", "arbitrary"),
        name="mla_prompt_attn",
    )(qlat, qpe, kv)


def _mla_decode_attn_kernel(pt_ref, qlat_ref, qpe_ref, kvnew_ref, *rest, n_pages, n_new):
    del pt_ref
    page_refs = rest[:n_pages]
    o_ref, kbuf, m_scr, l_scr, acc_scr = rest[n_pages:]
    s_id = pl.program_id(1)
    qlat = qlat_ref[0]
    qpe = qpe_ref[0]

    @pl.when(s_id == 0)
    def _():
        kn = kvnew_ref[0].astype(BF)
        s = _scores(qlat, qpe, kn)
        tok = lax.broadcasted_iota(jnp.int32, s.shape, 0) & (n_new - 1)
        col = lax.broadcasted_iota(jnp.int32, s.shape, 1)
        s = jnp.where(col <= tok, s, -jnp.inf)
        m = jnp.max(s, axis=-1, keepdims=True)
        p = jnp.exp(s - m)
        m_scr[...] = m
        l_scr[...] = jnp.sum(p, axis=-1, keepdims=True)
        acc_scr[...] = _dot(p, kn[:, :MLA_KV_LORA])

    for p_id in range(n_pages):
        kbuf[p_id * PAGE_SIZE:(p_id + 1) * PAGE_SIZE, :] = page_refs[p_id][...].astype(BF)
    kb = kbuf[...]
    _softmax_step(_scores(qlat, qpe, kb), kb, m_scr, l_scr, acc_scr)

    @pl.when(s_id == pl.num_programs(1) - 1)
    def _():
        o_ref[0] = acc_scr[...] / l_scr[...]


def _mla_decode_attn(page_table, qlat, qpe, kvnew, cache, *, layer, n_new, pages_per_step):
    n_req, n_pages_total = page_table.shape
    rows = qlat.shape[1]
    steps = n_pages_total // pages_per_step

    def page_map(p_id):
        return lambda r, s, pt: (layer, pt[r, s * pages_per_step + p_id], 0, 0)

    grid_spec = pltpu.PrefetchScalarGridSpec(
        num_scalar_prefetch=1,
        grid=(n_req, steps),
        in_specs=[
            pl.BlockSpec((1, rows, MLA_KV_LORA), lambda r, s, pt: (r, 0, 0)),
            pl.BlockSpec((1, rows, MLA_ROPE), lambda r, s, pt: (r, 0, 0)),
            pl.BlockSpec((1,) + kvnew.shape[1:], lambda r, s, pt: (r, 0, 0)),
        ] + [pl.BlockSpec((None, None, PAGE_SIZE, MLA_LAT), page_map(p_id))
             for p_id in range(pages_per_step)],
        out_specs=pl.BlockSpec((1, rows, MLA_KV_LORA), lambda r, s, pt: (r, 0, 0)),
        scratch_shapes=[pltpu.VMEM((pages_per_step * PAGE_SIZE, MLA_LAT), BF),
                        pltpu.VMEM((rows, 1), F32), pltpu.VMEM((rows, 1), F32),
                        pltpu.VMEM((rows, MLA_KV_LORA), F32)],
    )
    return pl.pallas_call(
        functools.partial(_mla_decode_attn_kernel, n_pages=pages_per_step, n_new=n_new),
        grid_spec=grid_spec,
        out_shape=jax.ShapeDtypeStruct((n_req, rows, MLA_KV_LORA), F32),
        compiler_params=_params("parallel", "arbitrary"),
        name="mla_decode_attn",
    )(page_table, qlat, qpe, kvnew, *([cache] * pages_per_step))


def _mla_out_kernel(x_ref, o_ref, wuv_ref, wout_ref, y_ref, ov_scr):
    tq = x_ref.shape[0]
    for h in range(MLA_HEADS):
        ov_scr[:, h * MLA_V:(h + 1) * MLA_V] = _dot(o_ref[0, h * tq:(h + 1) * tq, :],
                                                    wuv_ref[h]).astype(BF)
    y_ref[...] = x_ref[...] + _dot(ov_scr[...], wout_ref[...])


def _mla_out(h, o, wuv, wout, *, tq):
    t = h.shape[0]
    hq = MLA_HEADS * tq
    return pl.pallas_call(
        _mla_out_kernel,
        grid=(t // tq,),
        in_specs=[
            pl.BlockSpec((tq, D_MODEL), lambda i: (i, 0)),
            pl.BlockSpec((1, hq, MLA_KV_LORA), lambda i: (i, 0, 0)),
            _full((MLA_HEADS, MLA_KV_LORA, MLA_V)),
            _full((MLA_HEADS * MLA_V, D_MODEL)),
        ],
        out_specs=pl.BlockSpec((tq, D_MODEL), lambda i: (i, 0)),
        out_shape=jax.ShapeDtypeStruct((t, D_MODEL), F32),
        scratch_shapes=[pltpu.VMEM((tq, MLA_HEADS * MLA_V), BF)],
        compiler_params=_params("parallel"),
        name="mla_out",
    )(h, o, wuv, wout)


def _block_diag(x, w_ref, b):
    parts = [_dot(x[:, n * LRU_BLOCK_DIM:(n + 1) * LRU_BLOCK_DIM], w_ref[n])
             for n in range(LRU_BLOCKS)]
    return jnp.concatenate(parts, axis=-1) + b


def _lru_coeffs(xc, wa_ref, ba_ref, wi_ref, bi_ref, lam_ref):
    r = jax.nn.sigmoid(_block_diag(xc, wa_ref, ba_ref[...]))
    gi = jax.nn.sigmoid(_block_diag(xc, wi_ref, bi_ref[...]))
    neg_lam = -lam_ref[...]
    softplus = jnp.maximum(neg_lam, 0.0) + jnp.log1p(jnp.exp(-jnp.abs(neg_lam)))
    log_a = -LRU_C * r * softplus
    th = jnp.tanh(log_a)
    one_minus_a2 = -2.0 * th / (1.0 - th)
    return jnp.exp(log_a), jnp.sqrt(one_minus_a2) * gi * xc


def _mixer_c_prompt_kernel(x_ref, g_ref, wx_ref, wgate_ref, convw_ref, convb_ref, wa_ref, ba_ref,
                           wi_ref, bi_ref, lam_ref, wout_ref,
                           o_ref, hlast_ref, buf_ref,
                           xx_scr, gate_scr, a_scr, b_scr, hloc_scr, ploc_scr, carry_scr):
    tt = x_ref.shape[0]
    seg = tt // SUBLANES
    ti = pl.program_id(1)

    @pl.when(ti == 0)
    def _():
        xx_scr[0:SUBLANES, :] = jnp.zeros((SUBLANES, LRU_WIDTH), F32)
        carry_scr[...] = jnp.zeros_like(carry_scr)

    x = x_ref[...]
    xn = _rmsnorm(x, g_ref[...]).astype(BF)
    gate_scr[...] = jax.nn.gelu(_dot(xn, wgate_ref[...]))
    xx_scr[SUBLANES:, :] = _dot(xn, wx_ref[...])
    xc = convb_ref[...]
    for k in range(CONV_WIDTH):
        off = SUBLANES - (CONV_WIDTH - 1) + k
        xc = xc + convw_ref[k:k + 1, :] * xx_scr[off:off + tt, :]
    buf_ref[0] = xx_scr[tt + SUBLANES - (CONV_WIDTH - 1):, :]
    xx_scr[0:SUBLANES, :] = xx_scr[tt:tt + SUBLANES, :]

    a, b = _lru_coeffs(xc, wa_ref, ba_ref, wi_ref, bi_ref, lam_ref)
    for n in range(LRU_BLOCKS):
        cols = slice(n * LRU_BLOCK_DIM, (n + 1) * LRU_BLOCK_DIM)
        a_scr[n] = a[:, cols]
        b_scr[n] = b[:, cols]

    def body(j, carry):
        hs, ps = carry
        rows = pl.ds(j, SUBLANES, stride=seg)
        new_h, new_p = [], []
        for n in range(LRU_BLOCKS):
            aj = a_scr[n, rows, :]
            h = aj * hs[n] + b_scr[n, rows, :]
            p = aj * ps[n]
            hloc_scr[n, rows, :] = h
            ploc_scr[n, rows, :] = p
            new_h.append(h)
            new_p.append(p)
        return tuple(new_h), tuple(new_p)

    h_end, p_end = lax.fori_loop(
        0, seg, body,
        ((jnp.zeros((SUBLANES, LRU_BLOCK_DIM), F32),) * LRU_BLOCKS,
         (jnp.ones((SUBLANES, LRU_BLOCK_DIM), F32),) * LRU_BLOCKS))

    for n in range(LRU_BLOCKS):
        cols = slice(n * LRU_BLOCK_DIM, (n + 1) * LRU_BLOCK_DIM)
        c = carry_scr[:, cols]
        for k in range(SUBLANES):
            rows = slice(k * seg, (k + 1) * seg)
            gate_scr[rows, cols] = gate_scr[rows, cols] * (hloc_scr[n, rows, :]
                                                           + ploc_scr[n, rows, :] * c)
            c = p_end[n][k:k + 1, :] * c + h_end[n][k:k + 1, :]
        carry_scr[:, cols] = c
    hlast_ref[0] = carry_scr[...]
    o_ref[...] = x + _dot(gate_scr[...], wout_ref[...])


def _mixer_c_prompt(h, g, w, *, n_batch, seq, tt):
    nt = seq // tt
    wd = LRU_WIDTH
    blk = (LRU_BLOCKS, LRU_BLOCK_DIM, LRU_BLOCK_DIM)
    return pl.pallas_call(
        _mixer_c_prompt_kernel,
        grid=(n_batch, nt),
        in_specs=[
            pl.BlockSpec((tt, D_MODEL), lambda b, i: (b * nt + i, 0)),
            _full((1, D_MODEL)),
            _full((D_MODEL, wd)), _full((D_MODEL, wd)),
            _full((CONV_WIDTH, wd)), _full((1, wd)),
            _full(blk), _full((1, wd)), _full(blk), _full((1, wd)), _full((1, wd)),
            _full((wd, D_MODEL)),
        ],
        out_specs=[pl.BlockSpec((tt, D_MODEL), lambda b, i: (b * nt + i, 0)),
                   pl.BlockSpec((1, 1, wd), lambda b, i: (b, 0, 0)),
                   pl.BlockSpec((1, CONV_WIDTH - 1, wd), lambda b, i: (b, 0, 0))],
        out_shape=[jax.ShapeDtypeStruct((n_batch * seq, D_MODEL), F32),
                   jax.ShapeDtypeStruct((n_batch, 1, wd), F32),
                   jax.ShapeDtypeStruct((n_batch, CONV_WIDTH - 1, wd), F32)],
        scratch_shapes=[pltpu.VMEM((tt + SUBLANES, wd), F32), pltpu.VMEM((tt, wd), F32)] +
                       [pltpu.VMEM((LRU_BLOCKS, tt, LRU_BLOCK_DIM), F32)] * 4 +
                       [pltpu.VMEM((1, wd), F32)],
        compiler_params=_params("parallel", "arbitrary"),
        name="mixer_c_prompt",
    )(h, g, w["wx"], w["wgate"], w["convw"], w["convb"], w["wa"], w["ba"], w["wi"], w["bi"],
      w["lam"], w["wout"])


def _mixer_c_sample_kernel(x_ref, g_ref, h0_ref, buf0_ref, wx_ref, wgate_ref, convw_ref, convb_ref,
                           wa_ref, ba_ref, wi_ref, bi_ref, lam_ref, wout_ref,
                           o_ref, hlast_ref, buf_ref, xx_scr, gh_scr, *, n_req):
    t = x_ref.shape[0]
    hist = (CONV_WIDTH - 1) * n_req
    x = x_ref[...]
    xn = _rmsnorm(x, g_ref[...]).astype(BF)
    gate = jax.nn.gelu(_dot(xn, wgate_ref[...]))
    xx_scr[0:hist, :] = buf0_ref[...]
    xx_scr[hist:, :] = _dot(xn, wx_ref[...])
    xc = convb_ref[...]
    for k in range(CONV_WIDTH):
        xc = xc + convw_ref[k:k + 1, :] * xx_scr[k * n_req:k * n_req + t, :]
    buf_ref[...] = xx_scr[t:, :]
    a, b = _lru_coeffs(xc, wa_ref, ba_ref, wi_ref, bi_ref, lam_ref)
    h = h0_ref[...]
    for step in range(t // n_req):
        rows = slice(step * n_req, (step + 1) * n_req)
        h = a[rows, :] * h + b[rows, :]
        gh_scr[rows, :] = gate[rows, :] * h
    hlast_ref[...] = h
    o_ref[...] = x + _dot(gh_scr[...], wout_ref[...])


def _mixer_c_sample(h, g, h0, buf0, w, *, n_req):
    t = h.shape[0]
    wd = LRU_WIDTH
    hist = (CONV_WIDTH - 1) * n_req
    blk = (LRU_BLOCKS, LRU_BLOCK_DIM, LRU_BLOCK_DIM)
    return pl.pallas_call(
        functools.partial(_mixer_c_sample_kernel, n_req=n_req),
        grid=(1,),
        in_specs=[
            _full((t, D_MODEL)), _full((1, D_MODEL)), _full((n_req, wd)), _full((hist, wd)),
            _full((D_MODEL, wd)), _full((D_MODEL, wd)),
            _full((CONV_WIDTH, wd)), _full((1, wd)),
            _full(blk), _full((1, wd)), _full(blk), _full((1, wd)), _full((1, wd)),
            _full((wd, D_MODEL)),
        ],
        out_specs=[_full((t, D_MODEL)), _full((n_req, wd)), _full((hist, wd))],
        out_shape=[jax.ShapeDtypeStruct((t, D_MODEL), F32),
                   jax.ShapeDtypeStruct((n_req, wd), F32),
                   jax.ShapeDtypeStruct((hist, wd), F32)],
        scratch_shapes=[pltpu.VMEM((hist + t, wd), F32), pltpu.VMEM((t, wd), F32)],
        compiler_params=_params("arbitrary"),
        name="mixer_c_sample",
    )(h, g, h0, buf0, w["wx"], w["wgate"], w["convw"], w["convb"], w["wa"], w["ba"], w["wi"],
      w["bi"], w["lam"], w["wout"])


def _rope_tables(pos):
    inv = 1.0 / (ROPE_THETA ** (jnp.arange(0, MLA_ROPE, 2, dtype=F32) / MLA_ROPE))
    ang = pos.astype(F32)[:, None] * inv[None, :]
    cos, sin = jnp.cos(ang), jnp.sin(ang)
    cos_t = jnp.tile(jnp.concatenate([cos, cos], axis=-1), (1, MLA_HEADS))
    sin_t = jnp.tile(jnp.concatenate([-sin, sin], axis=-1), (1, MLA_HEADS))
    return cos_t, sin_t


def _swap_halves(w):
    half = w.shape[-1] // 2
    return jnp.concatenate([w[..., half:], w[..., :half]], axis=-1)


def kernel(x_prompt, x_sample, cache_mla, state_lru_h, state_lru_conv, page_table, norm_mix, norm_ffn, norm_out, a_w_in, a_ln_g, a_ln_b, a_w_s, a_b_s, a_w_out, b_w_dq, b_q_norm, b_w_uq, b_w_dkv, b_kv_norm, b_w_uk, b_w_uv, b_w_out, c_w_x, c_w_gate, c_conv_w, c_conv_b, c_w_a, c_b_a, c_w_i, c_b_i, c_lambda, c_w_out, f_w_gate, f_w_up, f_w_down):
    n_pr, len_p, _ = x_prompt.shape
    n_dec, len_s, _ = x_sample.shape
    depth = norm_mix.shape[0]
    past_len = page_table.shape[1] * PAGE_SIZE
    t_s = n_dec * len_s
    assert len_p % A_CHUNK == 0 and len_s <= A_CHUNK and (len_s & (len_s - 1)) == 0

    def row(v):
        return v.reshape(1, -1)

    def to_tm(v):
        return jnp.swapaxes(v, 0, 1).reshape((t_s,) + v.shape[2:])

    def from_tm(v):
        return jnp.swapaxes(v.reshape((len_s, n_dec) + v.shape[1:]), 0, 1)

    hp = x_prompt.reshape(n_pr * len_p, D_MODEL)
    hs = to_tm(x_sample)

    cos_p, sin_p = _rope_tables(jnp.arange(len_p, dtype=jnp.int32))
    pos_s = past_len + jnp.arange(len_s, dtype=jnp.int32)
    cos_s, sin_s = _rope_tables(jnp.repeat(pos_s, n_dec))

    tril = jnp.tril(jnp.ones((A_CHUNK, A_CHUNK), dtype=bool))
    eye_req = jnp.eye(n_dec, dtype=F32)

    chunk_v_s, mla_p, mla_s = [], [], []
    lru_h_p, lru_h_s, lru_c_p, lru_c_s = [], [], [], []
    for layer in range(depth):
        kind, j = layer % 3, layer // 3
        g_mix = row(norm_mix[layer])
        if kind == 0:
            w_causal = jnp.where(tril[None], a_w_s[j], 0.0)
            w_small = w_causal[:, :len_s, :len_s]
            wmix_s = jnp.einsum("gts,rq->gtrsq", w_small, eye_req).reshape(A_GROUPS, t_s, t_s)
            pad = A_CHUNK - t_s
            wmix_s = jnp.pad(wmix_s, ((0, 0), (0, pad), (0, pad)))
            bmix_p = a_b_s[j].T
            bmix_s = jnp.pad(jnp.repeat(a_b_s[j].T[:len_s], n_dec, axis=0), ((0, pad), (0, 0)))
            args = (a_w_in[j].astype(BF), row(a_ln_g[j]), row(a_ln_b[j]))
            w_out = a_w_out[j].astype(BF)
            hp, _ = _mixer_a(hp, g_mix, *args, w_causal.astype(BF), bmix_p, w_out, tm=512)
            hs, v_s = _mixer_a(hs, g_mix, *args, wmix_s.astype(BF), bmix_s, w_out, tm=t_s)
            chunk_v_s.append(from_tm(v_s))
        elif kind == 1:
            nope_cols = (jnp.arange(MLA_HEADS)[:, None] * (MLA_NOPE + MLA_ROPE)
                         + jnp.arange(MLA_NOPE)[None, :]).reshape(-1)
            pe_cols = (jnp.arange(MLA_HEADS)[:, None] * (MLA_NOPE + MLA_ROPE) + MLA_NOPE
                       + jnp.arange(MLA_ROPE)[None, :])
            pe_cols_sw = _swap_halves(pe_cols)
            w = dict(
                wdq=b_w_dq[j].astype(BF), qn=row(b_q_norm[j]),
                wuqn=b_w_uq[j][:, nope_cols].astype(BF),
                wuqa=b_w_uq[j][:, pe_cols.reshape(-1)].astype(BF),
                wuqb=b_w_uq[j][:, pe_cols_sw.reshape(-1)].astype(BF),
                wdkv=b_w_dkv[j].astype(BF),
                wdkvb=_swap_halves(b_w_dkv[j][:, MLA_KV_LORA:]).astype(BF),
                kvn=row(b_kv_norm[j]),
                wuk=jnp.transpose(b_w_uk[j], (1, 2, 0)).astype(BF),
            )
            wuv = jnp.transpose(b_w_uv[j], (1, 0, 2)).astype(BF)
            wout = b_w_out[j].astype(BF)

            tq = 256
            kv_p, qlat_p, qpe_p = _mla_proj(hp, g_mix, cos_p, sin_p, w, tq=tq)
            o_p = _mla_prompt_attn(qlat_p, qpe_p, kv_p, n_batch=n_pr, seq=len_p, tq=tq, tk=256)
            hp = _mla_out(hp, o_p, wuv, wout, tq=tq)
            mla_p.append(kv_p.reshape(n_pr, len_p, MLA_LAT))

            kv_s, qlat_s, qpe_s = _mla_proj(hs, g_mix, cos_s, sin_s, w, tq=t_s)

            def per_request(q):
                q = q.reshape(MLA_HEADS, len_s, n_dec, q.shape[-1])
                return jnp.transpose(q, (2, 0, 1, 3)).reshape(n_dec, MLA_HEADS * len_s, q.shape[-1])

            kv_new = jnp.pad(from_tm(kv_s), ((0, 0), (0, 16 - len_s), (0, 0)))
            o_s = _mla_decode_attn(page_table, per_request(qlat_s), per_request(qpe_s), kv_new,
                                   cache_mla, layer=j, n_new=len_s, pages_per_step=16)
            o_s = o_s.reshape(n_dec, MLA_HEADS, len_s, MLA_KV_LORA)
            o_s = jnp.transpose(o_s, (1, 2, 0, 3)).reshape(1, MLA_HEADS * t_s, MLA_KV_LORA)
            hs = _mla_out(hs, o_s, wuv, wout, tq=t_s)
            mla_s.append(from_tm(kv_s))
        else:
            w = dict(
                wx=c_w_x[j].astype(BF), wgate=c_w_gate[j].astype(BF), convw=c_conv_w[j],
                convb=row(c_conv_b[j]), wa=c_w_a[j].astype(BF), ba=row(c_b_a[j]),
                wi=c_w_i[j].astype(BF), bi=row(c_b_i[j]), lam=row(c_lambda[j]),
                wout=c_w_out[j].astype(BF),
            )
            hp, hl_p, cb_p = _mixer_c_prompt(hp, g_mix, w, n_batch=n_pr, seq=len_p, tt=256)
            lru_h_p.append(hl_p.reshape(n_pr, LRU_WIDTH))
            lru_c_p.append(cb_p)
            buf0 = jnp.swapaxes(state_lru_conv[j], 0, 1).reshape(-1, LRU_WIDTH)
            hs, hl_s, cb_s = _mixer_c_sample(hs, g_mix, state_lru_h[j], buf0, w, n_req=n_dec)
            lru_h_s.append(hl_s)
            lru_c_s.append(jnp.swapaxes(cb_s.reshape(CONV_WIDTH - 1, n_dec, LRU_WIDTH), 0, 1))

        last = layer == depth - 1
        ffn_w = (row(norm_ffn[layer]), f_w_gate[layer], f_w_up[layer], f_w_down[layer],
                 row(norm_out))
        hp = _ffn(hp, *ffn_w, tm=1024, tf=256, final_norm=last)
        hs = _ffn(hs, *ffn_w, tm=t_s, tf=256, final_norm=last)

    return (hp.reshape(n_pr, len_p, D_MODEL), from_tm(hs), jnp.stack(chunk_v_s),
            jnp.stack(mla_p), jnp.stack(mla_s), jnp.stack(lru_h_p), jnp.stack(lru_h_s),
            jnp.stack(lru_c_p), jnp.stack(lru_c_s))
```

```python
import functools

import jax
import jax.numpy as jnp
from jax import lax
from jax.experimental import pallas as pl
from jax.experimental.pallas import tpu as pltpu

BF = jnp.bfloat16
F32 = jnp.float32

D_MODEL = 1024
D_FF = 2816
A_CHUNK = 128
A_GROUPS = 8
MLA_HEADS = 8
MLA_Q_LORA = 384
MLA_KV_LORA = 256
MLA_NOPE = 128
MLA_ROPE = 64
MLA_V = 128
MLA_LAT = MLA_KV_LORA + MLA_ROPE
MLA_SCALE = (MLA_NOPE + MLA_ROPE) ** -0.5
ROPE_THETA = 10000.0
PAGE_SIZE = 128
LRU_WIDTH = 1280
LRU_BLOCKS = 10
LRU_BLOCK_DIM = 128
CONV_WIDTH = 4
LRU_C = 8.0
RMS_EPS = 1e-6
LN_EPS = 1e-5

SUBLANES = 8
LANES = 128
LOG2E = 1.4426950408889634
VMEM_LIMIT = 56 * 1024 * 1024


def _params(*sem):
    return pltpu.CompilerParams(dimension_semantics=sem, vmem_limit_bytes=VMEM_LIMIT)


def _dot(a, b):
    return jnp.dot(a.astype(BF), b.astype(BF), preferred_element_type=F32)


def _dot_nt(a, b):
    return lax.dot_general(a.astype(BF), b.astype(BF), (((1,), (1,)), ((), ())),
                           preferred_element_type=F32)


def _rmsnorm(x, g):
    return x * lax.rsqrt(jnp.mean(x * x, axis=-1, keepdims=True) + RMS_EPS) * g


def _layernorm(x, g, b):
    mu = jnp.mean(x, axis=-1, keepdims=True)
    xc = x - mu
    return xc * lax.rsqrt(jnp.mean(xc * xc, axis=-1, keepdims=True) + LN_EPS) * g + b


def _sigmoid(x):
    return 0.5 * jnp.tanh(0.5 * x) + 0.5


def _full(shape):
    return pl.BlockSpec(shape, lambda *_: (0,) * len(shape))


def _ffn_kernel(x_ref, g_ref, wg_ref, wu_ref, wd_ref, gout_ref, o_ref, xn_ref, acc_ref, *,
                final_norm):
    j = pl.program_id(1)

    @pl.when(j == 0)
    def _():
        xn_ref[...] = _rmsnorm(x_ref[...], g_ref[...]).astype(BF)
        acc_ref[...] = jnp.zeros_like(acc_ref)

    xn = xn_ref[...]
    gate = _dot(xn, wg_ref[...])
    up = _dot(xn, wu_ref[...])
    acc_ref[...] += _dot(gate * _sigmoid(gate) * up, wd_ref[...])

    @pl.when(j == pl.num_programs(1) - 1)
    def _():
        y = x_ref[...] + acc_ref[...]
        if final_norm:
            y = _rmsnorm(y, gout_ref[...])
        o_ref[...] = y


def _ffn(h, g, wg, wu, wd, gout, *, layer, tm, tf, final_norm):
    t = h.shape[0]
    return pl.pallas_call(
        functools.partial(_ffn_kernel, final_norm=final_norm),
        grid=(t // tm, D_FF // tf),
        in_specs=[
            pl.BlockSpec((tm, D_MODEL), lambda i, j: (i, 0)),
            _full((1, D_MODEL)),
            pl.BlockSpec((None, D_MODEL, tf), lambda i, j: (layer, 0, j)),
            pl.BlockSpec((None, D_MODEL, tf), lambda i, j: (layer, 0, j)),
            pl.BlockSpec((None, tf, D_MODEL), lambda i, j: (layer, j, 0)),
            _full((1, D_MODEL)),
        ],
        out_specs=pl.BlockSpec((tm, D_MODEL), lambda i, j: (i, 0)),
        out_shape=jax.ShapeDtypeStruct((t, D_MODEL), F32),
        scratch_shapes=[pltpu.VMEM((tm, D_MODEL), BF), pltpu.VMEM((tm, D_MODEL), F32)],
        compiler_params=_params("parallel", "arbitrary"),
        name="ffn",
    )(h, g, wg, wu, wd, gout)


def _mixer_a_kernel(x_ref, g_ref, win_ref, lng_ref, lnb_ref, wmix_ref, bmix_ref, wout_ref,
                    o_ref, v_ref, u_scr, vb_scr, uv_scr):
    tm = x_ref.shape[0]
    x = x_ref[...]
    xn = _rmsnorm(x, g_ref[...]).astype(BF)
    u_scr[...] = jax.nn.gelu(_dot(xn, win_ref[:, :D_MODEL]))
    v = _layernorm(jax.nn.gelu(_dot(xn, win_ref[:, D_MODEL:])), lng_ref[...], lnb_ref[...])
    v_ref[...] = v
    vb_scr[...] = v.astype(BF)
    for c in range(tm // A_CHUNK):
        rows = slice(c * A_CHUNK, (c + 1) * A_CHUNK)
        for g in range(A_GROUPS):
            cols = slice(g * A_CHUNK, (g + 1) * A_CHUNK)
            mixed = _dot(wmix_ref[g], vb_scr[rows, cols]) + bmix_ref[:, g:g + 1]
            uv_scr[rows, cols] = (u_scr[rows, cols] * mixed).astype(BF)
    o_ref[...] = x + _dot(uv_scr[...], wout_ref[...])


def _mixer_a(h, g, win, lng, lnb, wmix, bmix, wout, *, tm):
    t = h.shape[0]
    return pl.pallas_call(
        _mixer_a_kernel,
        grid=(t // tm,),
        in_specs=[
            pl.BlockSpec((tm, D_MODEL), lambda i: (i, 0)),
            _full((1, D_MODEL)),
            _full((D_MODEL, 2 * D_MODEL)),
            _full((1, D_MODEL)),
            _full((1, D_MODEL)),
            _full((A_GROUPS, A_CHUNK, A_CHUNK)),
            _full((A_CHUNK, A_GROUPS)),
            _full((D_MODEL, D_MODEL)),
        ],
        out_specs=[pl.BlockSpec((tm, D_MODEL), lambda i: (i, 0)),
                   pl.BlockSpec((tm, D_MODEL), lambda i: (i, 0))],
        out_shape=[jax.ShapeDtypeStruct((t, D_MODEL), F32),
                   jax.ShapeDtypeStruct((t, D_MODEL), F32)],
        scratch_shapes=[pltpu.VMEM((tm, D_MODEL), F32), pltpu.VMEM((tm, D_MODEL), BF),
                        pltpu.VMEM((tm, D_MODEL), BF)],
        compiler_params=_params("parallel"),
        name="mixer_a",
    )(h, g, win, lng, lnb, wmix, bmix, wout)


def _mla_proj_kernel(x_ref, g_ref, cos_ref, sin_ref, wdq_ref, qn_ref, wuqn_ref, wuqa_ref,
                     wuqb_ref, wdkv_ref, wdkvb_ref, kvn_ref, wuk_ref,
                     kv_ref, qlat_ref, qpe_ref):
    tq = x_ref.shape[0]
    xn = _rmsnorm(x_ref[...], g_ref[...]).astype(BF)
    cq = _rmsnorm(_dot(xn, wdq_ref[...]), qn_ref[...]).astype(BF)
    q_nope = _dot(cq, wuqn_ref[...])
    cos = cos_ref[...]
    sin = sin_ref[...]
    q_pe = _dot(cq, wuqa_ref[...]) * cos + _dot(cq, wuqb_ref[...]) * sin
    kv = _dot(xn, wdkv_ref[...])
    kv_ref[:, :MLA_KV_LORA] = _rmsnorm(kv[:, :MLA_KV_LORA], kvn_ref[...])
    kv_ref[:, MLA_KV_LORA:] = (kv[:, MLA_KV_LORA:] * cos[:, :MLA_ROPE]
                               + _dot(xn, wdkvb_ref[...]) * sin[:, :MLA_ROPE])
    for h in range(MLA_HEADS):
        rows = slice(h * tq, (h + 1) * tq)
        qlat_ref[0, rows, :] = _dot(q_nope[:, h * MLA_NOPE:(h + 1) * MLA_NOPE],
                                    wuk_ref[h]).astype(BF)
        qpe_ref[0, rows, :] = q_pe[:, h * MLA_ROPE:(h + 1) * MLA_ROPE].astype(BF)


def _mla_proj(h, g, cos, sin, w, *, tq):
    t = h.shape[0]
    nb = t // tq
    npos = cos.shape[0] // tq
    hq = MLA_HEADS * tq
    return pl.pallas_call(
        _mla_proj_kernel,
        grid=(nb,),
        in_specs=[
            pl.BlockSpec((tq, D_MODEL), lambda i: (i, 0)),
            _full((1, D_MODEL)),
            pl.BlockSpec((tq, MLA_HEADS * MLA_ROPE), lambda i: (i % npos, 0)),
            pl.BlockSpec((tq, MLA_HEADS * MLA_ROPE), lambda i: (i % npos, 0)),
            _full((D_MODEL, MLA_Q_LORA)),
            _full((1, MLA_Q_LORA)),
            _full((MLA_Q_LORA, MLA_HEADS * MLA_NOPE)),
            _full((MLA_Q_LORA, MLA_HEADS * MLA_ROPE)),
            _full((MLA_Q_LORA, MLA_HEADS * MLA_ROPE)),
            _full((D_MODEL, MLA_LAT)),
            _full((D_MODEL, MLA_ROPE)),
            _full((1, MLA_KV_LORA)),
            _full((MLA_HEADS, MLA_NOPE, MLA_KV_LORA)),
        ],
        out_specs=[pl.BlockSpec((tq, MLA_LAT), lambda i: (i, 0)),
                   pl.BlockSpec((1, hq, MLA_KV_LORA), lambda i: (i, 0, 0)),
                   pl.BlockSpec((1, hq, MLA_ROPE), lambda i: (i, 0, 0))],
        out_shape=[jax.ShapeDtypeStruct((t, MLA_LAT), F32),
                   jax.ShapeDtypeStruct((nb, hq, MLA_KV_LORA), BF),
                   jax.ShapeDtypeStruct((nb, hq, MLA_ROPE), BF)],
        compiler_params=_params("parallel"),
        name="mla_proj",
    )(h, g, cos, sin, w["wdq"], w["qn"], w["wuqn"], w["wuqa"], w["wuqb"], w["wdkv"],
      w["wdkvb"], w["kvn"], w["wuk"])


def _lanes(x, n):
    return x[:, :n] if n <= LANES else jnp.concatenate([x] * (n // LANES), axis=-1)


def _softmax_step(s, pv, m_scr, l_scr, acc_scr, rows):
    c2 = MLA_SCALE * LOG2E
    m_prev = m_scr[rows, :]
    m_new = jnp.maximum(m_prev, jnp.max(s, axis=-1, keepdims=True))
    alpha = jnp.exp2((m_prev - m_new) * c2)
    p = jnp.exp2((s - _lanes(m_new, s.shape[-1])) * c2)
    l_scr[rows, :] = alpha * l_scr[rows, :] + jnp.sum(p, axis=-1, keepdims=True)
    acc_scr[rows, :] = _lanes(alpha, MLA_KV_LORA) * acc_scr[rows, :] + pv(p)
    m_scr[rows, :] = m_new


def _mla_prompt_attn_kernel(qlat_ref, qpe_ref, kv_ref, o_ref, m_scr, l_scr, acc_scr, *, tq, tk):
    i = pl.program_id(1)
    j = pl.program_id(2)
    j_diag = ((i + 1) * tq - 1) // tk

    @pl.when(j == 0)
    def _():
        m_scr[...] = jnp.full_like(m_scr, -jnp.inf)
        l_scr[...] = jnp.zeros_like(l_scr)
        acc_scr[...] = jnp.zeros_like(acc_scr)

    def step(masked):
        kb = kv_ref[...].astype(BF)
        klat, kpe = kb[:, :MLA_KV_LORA], kb[:, MLA_KV_LORA:]
        if masked:
            q_pos = i * tq + lax.broadcasted_iota(jnp.int32, (tq, tk), 0)
            k_pos = j * tk + lax.broadcasted_iota(jnp.int32, (tq, tk), 1)
            visible = k_pos <= q_pos
        for h in range(MLA_HEADS):
            rows = slice(h * tq, (h + 1) * tq)
            s = _dot_nt(qlat_ref[0, rows, :], klat) + _dot_nt(qpe_ref[0, rows, :], kpe)
            if masked:
                s = jnp.where(visible, s, -jnp.inf)
            _softmax_step(s, lambda p: _dot(p, klat), m_scr, l_scr, acc_scr, rows)

    @pl.when(j < j_diag)
    def _():
        step(False)

    @pl.when(j == j_diag)
    def _():
        step(True)

    @pl.when(j == pl.num_programs(2) - 1)
    def _():
        o_ref[0] = (acc_scr[...] / _lanes(l_scr[...], MLA_KV_LORA)).astype(o_ref.dtype)


def _mla_prompt_attn(qlat, qpe, kv, *, n_batch, seq, tq, tk):
    nq, nk = seq // tq, seq // tk
    hq = MLA_HEADS * tq

    def kv_map(b, i, j):
        return (b * nk + jnp.minimum(j, ((i + 1) * tq - 1) // tk), 0)

    return pl.pallas_call(
        functools.partial(_mla_prompt_attn_kernel, tq=tq, tk=tk),
        grid=(n_batch, nq, nk),
        in_specs=[
            pl.BlockSpec((1, hq, MLA_KV_LORA), lambda b, i, j: (b * nq + i, 0, 0)),
            pl.BlockSpec((1, hq, MLA_ROPE), lambda b, i, j: (b * nq + i, 0, 0)),
            pl.BlockSpec((tk, MLA_LAT), kv_map),
        ],
        out_specs=pl.BlockSpec((1, hq, MLA_KV_LORA), lambda b, i, j: (b * nq + i, 0, 0)),
        out_shape=jax.ShapeDtypeStruct((n_batch * nq, hq, MLA_KV_LORA), BF),
        scratch_shapes=[pltpu.VMEM((hq, LANES), F32), pltpu.VMEM((hq, LANES), F32),
                        pltpu.VMEM((hq, MLA_KV_LORA), F32)],
        compiler_params=_params("parallel", "parallel", "arbitrary"),
        name="mla_prompt_attn",
    )(qlat, qpe, kv)


def _mla_decode_attn_kernel(pt_ref, qlat_ref, qpe_ref, kvnew_ref, *rest, n_pages, n_new):
    del pt_ref
    page_refs = rest[:n_pages]
    o_ref, kbuf, m_scr, l_scr, acc_scr = rest[n_pages:]
    s_id = pl.program_id(1)
    qlat = qlat_ref[0]
    qpe = qpe_ref[0]
    every_row = slice(None)

    @pl.when(s_id == 0)
    def _():
        m_scr[...] = jnp.full_like(m_scr, -jnp.inf)
        l_scr[...] = jnp.zeros_like(l_scr)
        acc_scr[...] = jnp.zeros_like(acc_scr)
        kn = kvnew_ref[0].astype(BF)
        klat, kpe = kn[:, :MLA_KV_LORA], kn[:, MLA_KV_LORA:]
        s = _dot_nt(qlat, klat) + _dot_nt(qpe, kpe)
        tok = lax.broadcasted_iota(jnp.int32, s.shape, 0) & (n_new - 1)
        col = lax.broadcasted_iota(jnp.int32, s.shape, 1)
        s = jnp.where(col <= tok, s, -jnp.inf)
        _softmax_step(s, lambda p: _dot(p, klat), m_scr, l_scr, acc_scr, every_row)

    for p_id in range(n_pages):
        kbuf[:, p_id * PAGE_SIZE:(p_id + 1) * PAGE_SIZE] = page_refs[p_id][...].astype(BF)
    klat_t = kbuf[:MLA_KV_LORA, :]
    s = _dot(qlat, klat_t) + _dot(qpe, kbuf[MLA_KV_LORA:, :])
    _softmax_step(s, lambda p: _dot_nt(p, klat_t), m_scr, l_scr, acc_scr, every_row)

    @pl.when(s_id == pl.num_programs(1) - 1)
    def _():
        o_ref[0] = acc_scr[...] / _lanes(l_scr[...], MLA_KV_LORA)


def _mla_decode_attn(page_table, qlat, qpe, kvnew, cache_t, *, layer, n_new, pages_per_step):
    n_req, n_pages_total = page_table.shape
    rows = qlat.shape[1]
    steps = n_pages_total // pages_per_step

    def page_map(p_id):
        return lambda r, s, pt: (layer, pt[r, s * pages_per_step + p_id], 0, 0)

    grid_spec = pltpu.PrefetchScalarGridSpec(
        num_scalar_prefetch=1,
        grid=(n_req, steps),
        in_specs=[
            pl.BlockSpec((1, rows, MLA_KV_LORA), lambda r, s, pt: (r, 0, 0)),
            pl.BlockSpec((1, rows, MLA_ROPE), lambda r, s, pt: (r, 0, 0)),
            pl.BlockSpec((1,) + kvnew.shape[1:], lambda r, s, pt: (r, 0, 0)),
        ] + [pl.BlockSpec((None, None, MLA_LAT, PAGE_SIZE), page_map(p_id))
             for p_id in range(pages_per_step)],
        out_specs=pl.BlockSpec((1, rows, MLA_KV_LORA), lambda r, s, pt: (r, 0, 0)),
        scratch_shapes=[pltpu.VMEM((MLA_LAT, pages_per_step * PAGE_SIZE), BF),
                        pltpu.VMEM((rows, LANES), F32), pltpu.VMEM((rows, LANES), F32),
                        pltpu.VMEM((rows, MLA_KV_LORA), F32)],
    )
    return pl.pallas_call(
        functools.partial(_mla_decode_attn_kernel, n_pages=pages_per_step, n_new=n_new),
        grid_spec=grid_spec,
        out_shape=jax.ShapeDtypeStruct((n_req, rows, MLA_KV_LORA), F32),
        compiler_params=_params("parallel", "arbitrary"),
        name="mla_decode_attn",
    )(page_table, qlat, qpe, kvnew, *([cache_t] * pages_per_step))


def _mla_out_kernel(x_ref, o_ref, wuv_ref, wout_ref, y_ref, ov_scr):
    tq = x_ref.shape[0]
    for h in range(MLA_HEADS):
        ov_scr[:, h * MLA_V:(h + 1) * MLA_V] = _dot(o_ref[0, h * tq:(h + 1) * tq, :],
                                                    wuv_ref[h]).astype(BF)
    y_ref[...] = x_ref[...] + _dot(ov_scr[...], wout_ref[...])


def _mla_out(h, o, wuv, wout, *, tq):
    t = h.shape[0]
    hq = MLA_HEADS * tq
    return pl.pallas_call(
        _mla_out_kernel,
        grid=(t // tq,),
        in_specs=[
            pl.BlockSpec((tq, D_MODEL), lambda i: (i, 0)),
            pl.BlockSpec((1, hq, MLA_KV_LORA), lambda i: (i, 0, 0)),
            _full((MLA_HEADS, MLA_KV_LORA, MLA_V)),
            _full((MLA_HEADS * MLA_V, D_MODEL)),
        ],
        out_specs=pl.BlockSpec((tq, D_MODEL), lambda i: (i, 0)),
        out_shape=jax.ShapeDtypeStruct((t, D_MODEL), F32),
        scratch_shapes=[pltpu.VMEM((tq, MLA_HEADS * MLA_V), BF)],
        compiler_params=_params("parallel"),
        name="mla_out",
    )(h, o, wuv, wout)


def _block_diag(x, w_ref, b):
    parts = [_dot(x[:, n * LRU_BLOCK_DIM:(n + 1) * LRU_BLOCK_DIM], w_ref[n])
             for n in range(LRU_BLOCKS)]
    return jnp.concatenate(parts, axis=-1) + b


def _lru_coeffs(xc, wa_ref, ba_ref, wi_ref, bi_ref, lam_ref):
    r = _sigmoid(_block_diag(xc, wa_ref, ba_ref[...]))
    gi = _sigmoid(_block_diag(xc, wi_ref, bi_ref[...]))
    neg_lam = -lam_ref[...]
    softplus = jnp.maximum(neg_lam, 0.0) + jnp.log1p(jnp.exp(-jnp.abs(neg_lam)))
    a = jnp.exp((-LRU_C * softplus) * r)
    return a, jnp.sqrt(1.0 - a * a) * gi * xc


def _scan_pitch(seg):
    pitch = seg + SUBLANES
    return pitch if (pitch // SUBLANES) % 2 else pitch + SUBLANES


def _mixer_c_prompt_kernel(x_ref, g_ref, wx_ref, wgate_ref, convw_ref, convb_ref, wa_ref, ba_ref,
                           wi_ref, bi_ref, lam_ref, wout_ref,
                           o_ref, hlast_ref, buf_ref,
                           xx_scr, gate_scr, a_scr, b_scr, hloc_scr, ploc_scr, carry_scr):
    tt = x_ref.shape[0]
    seg = tt // SUBLANES
    pitch = _scan_pitch(seg)
    ti = pl.program_id(1)

    @pl.when(ti == 0)
    def _():
        xx_scr[0:SUBLANES, :] = jnp.zeros((SUBLANES, LRU_WIDTH), F32)
        carry_scr[...] = jnp.zeros_like(carry_scr)

    x = x_ref[...]
    xn = _rmsnorm(x, g_ref[...]).astype(BF)
    gate_scr[...] = jax.nn.gelu(_dot(xn, wgate_ref[...]))
    xx_scr[SUBLANES:, :] = _dot(xn, wx_ref[...])
    xc = convb_ref[...]
    for k in range(CONV_WIDTH):
        off = SUBLANES - (CONV_WIDTH - 1) + k
        xc = xc + convw_ref[k:k + 1, :] * xx_scr[off:off + tt, :]
    buf_ref[0] = xx_scr[tt + SUBLANES - (CONV_WIDTH - 1):, :]
    xx_scr[0:SUBLANES, :] = xx_scr[tt:tt + SUBLANES, :]

    a, b = _lru_coeffs(xc, wa_ref, ba_ref, wi_ref, bi_ref, lam_ref)
    for n in range(LRU_BLOCKS):
        cols = slice(n * LRU_BLOCK_DIM, (n + 1) * LRU_BLOCK_DIM)
        for k in range(SUBLANES):
            a_scr[n, k * pitch:k * pitch + seg, :] = a[k * seg:(k + 1) * seg, cols]
            b_scr[n, k * pitch:k * pitch + seg, :] = b[k * seg:(k + 1) * seg, cols]

    def body(j, carry):
        hs, ps = carry
        rows = pl.ds(j, SUBLANES, stride=pitch)
        new_h, new_p = [], []
        for n in range(LRU_BLOCKS):
            aj = a_scr[n, rows, :]
            h = aj * hs[n] + b_scr[n, rows, :]
            p = aj * ps[n]
            hloc_scr[n, rows, :] = h
            ploc_scr[n, rows, :] = p
            new_h.append(h)
            new_p.append(p)
        return tuple(new_h), tuple(new_p)

    h_end, p_end = lax.fori_loop(
        0, seg, body,
        ((jnp.zeros((SUBLANES, LRU_BLOCK_DIM), F32),) * LRU_BLOCKS,
         (jnp.ones((SUBLANES, LRU_BLOCK_DIM), F32),) * LRU_BLOCKS))

    for n in range(LRU_BLOCKS):
        cols = slice(n * LRU_BLOCK_DIM, (n + 1) * LRU_BLOCK_DIM)
        c = carry_scr[:, cols]
        for k in range(SUBLANES):
            rows = slice(k * seg, (k + 1) * seg)
            srows = slice(k * pitch, k * pitch + seg)
            gate_scr[rows, cols] = gate_scr[rows, cols] * (hloc_scr[n, srows, :]
                                                           + ploc_scr[n, srows, :] * c)
            c = p_end[n][k:k + 1, :] * c + h_end[n][k:k + 1, :]
        carry_scr[:, cols] = c
    hlast_ref[0] = carry_scr[...]
    o_ref[...] = x + _dot(gate_scr[...], wout_ref[...])


def _mixer_c_prompt(h, g, w, *, n_batch, seq, tt):
    nt = seq // tt
    wd = LRU_WIDTH
    blk = (LRU_BLOCKS, LRU_BLOCK_DIM, LRU_BLOCK_DIM)
    return pl.pallas_call(
        _mixer_c_prompt_kernel,
        grid=(n_batch, nt),
        in_specs=[
            pl.BlockSpec((tt, D_MODEL), lambda b, i: (b * nt + i, 0)),
            _full((1, D_MODEL)),
            _full((D_MODEL, wd)), _full((D_MODEL, wd)),
            _full((CONV_WIDTH, wd)), _full((1, wd)),
            _full(blk), _full((1, wd)), _full(blk), _full((1, wd)), _full((1, wd)),
            _full((wd, D_MODEL)),
        ],
        out_specs=[pl.BlockSpec((tt, D_MODEL), lambda b, i: (b * nt + i, 0)),
                   pl.BlockSpec((1, 1, wd), lambda b, i: (b, 0, 0)),
                   pl.BlockSpec((1, CONV_WIDTH - 1, wd), lambda b, i: (b, 0, 0))],
        out_shape=[jax.ShapeDtypeStruct((n_batch * seq, D_MODEL), F32),
                   jax.ShapeDtypeStruct((n_batch, 1, wd), F32),
                   jax.ShapeDtypeStruct((n_batch, CONV_WIDTH - 1, wd), F32)],
        scratch_shapes=[pltpu.VMEM((tt + SUBLANES, wd), F32), pltpu.VMEM((tt, wd), F32)] +
                       [pltpu.VMEM((LRU_BLOCKS, SUBLANES * _scan_pitch(tt // SUBLANES),
                                    LRU_BLOCK_DIM), F32)] * 4 +
                       [pltpu.VMEM((1, wd), F32)],
        compiler_params=_params("parallel", "arbitrary"),
        name="mixer_c_prompt",
    )(h, g, w["wx"], w["wgate"], w["convw"], w["convb"], w["wa"], w["ba"], w["wi"], w["bi"],
      w["lam"], w["wout"])


def _mixer_c_sample_kernel(x_ref, g_ref, h0_ref, buf0_ref, wx_ref, wgate_ref, convw_ref, convb_ref,
                           wa_ref, ba_ref, wi_ref, bi_ref, lam_ref, wout_ref,
                           o_ref, hlast_ref, buf_ref, xx_scr, gh_scr, *, n_req):
    t = x_ref.shape[0]
    hist = (CONV_WIDTH - 1) * n_req
    x = x_ref[...]
    xn = _rmsnorm(x, g_ref[...]).astype(BF)
    gate = jax.nn.gelu(_dot(xn, wgate_ref[...]))
    xx_scr[0:hist, :] = buf0_ref[...]
    xx_scr[hist:, :] = _dot(xn, wx_ref[...])
    xc = convb_ref[...]
    for k in range(CONV_WIDTH):
        xc = xc + convw_ref[k:k + 1, :] * xx_scr[k * n_req:k * n_req + t, :]
    buf_ref[...] = xx_scr[t:, :]
    a, b = _lru_coeffs(xc, wa_ref, ba_ref, wi_ref, bi_ref, lam_ref)
    h = h0_ref[...]
    for step in range(t // n_req):
        rows = slice(step * n_req, (step + 1) * n_req)
        h = a[rows, :] * h + b[rows, :]
        gh_scr[rows, :] = gate[rows, :] * h
    hlast_ref[...] = h
    o_ref[...] = x + _dot(gh_scr[...], wout_ref[...])


def _mixer_c_sample(h, g, h0, buf0, w, *, n_req):
    t = h.shape[0]
    wd = LRU_WIDTH
    hist = (CONV_WIDTH - 1) * n_req
    blk = (LRU_BLOCKS, LRU_BLOCK_DIM, LRU_BLOCK_DIM)
    return pl.pallas_call(
        functools.partial(_mixer_c_sample_kernel, n_req=n_req),
        grid=(1,),
        in_specs=[
            _full((t, D_MODEL)), _full((1, D_MODEL)), _full((n_req, wd)), _full((hist, wd)),
            _full((D_MODEL, wd)), _full((D_MODEL, wd)),
            _full((CONV_WIDTH, wd)), _full((1, wd)),
            _full(blk), _full((1, wd)), _full(blk), _full((1, wd)), _full((1, wd)),
            _full((wd, D_MODEL)),
        ],
        out_specs=[_full((t, D_MODEL)), _full((n_req, wd)), _full((hist, wd))],
        out_shape=[jax.ShapeDtypeStruct((t, D_MODEL), F32),
                   jax.ShapeDtypeStruct((n_req, wd), F32),
                   jax.ShapeDtypeStruct((hist, wd), F32)],
        scratch_shapes=[pltpu.VMEM((hist + t, wd), F32), pltpu.VMEM((t, wd), F32)],
        compiler_params=_params("arbitrary"),
        name="mixer_c_sample",
    )(h, g, h0, buf0, w["wx"], w["wgate"], w["convw"], w["convb"], w["wa"], w["ba"], w["wi"],
      w["bi"], w["lam"], w["wout"])


def _rope_tables(pos):
    inv = 1.0 / (ROPE_THETA ** (jnp.arange(0, MLA_ROPE, 2, dtype=F32) / MLA_ROPE))
    ang = pos.astype(F32)[:, None] * inv[None, :]
    cos, sin = jnp.cos(ang), jnp.sin(ang)
    cos_t = jnp.tile(jnp.concatenate([cos, cos], axis=-1), (1, MLA_HEADS))
    sin_t = jnp.tile(jnp.concatenate([-sin, sin], axis=-1), (1, MLA_HEADS))
    return cos_t, sin_t


def _swap_halves(w):
    half = w.shape[-1] // 2
    return jnp.concatenate([w[..., half:], w[..., :half]], axis=-1)


def kernel(x_prompt, x_sample, cache_mla, state_lru_h, state_lru_conv, page_table, norm_mix, norm_ffn, norm_out, a_w_in, a_ln_g, a_ln_b, a_w_s, a_b_s, a_w_out, b_w_dq, b_q_norm, b_w_uq, b_w_dkv, b_kv_norm, b_w_uk, b_w_uv, b_w_out, c_w_x, c_w_gate, c_conv_w, c_conv_b, c_w_a, c_b_a, c_w_i, c_b_i, c_lambda, c_w_out, f_w_gate, f_w_up, f_w_down):
    n_pr, len_p, _ = x_prompt.shape
    n_dec, len_s, _ = x_sample.shape
    depth = norm_mix.shape[0]
    past_len = page_table.shape[1] * PAGE_SIZE
    t_s = n_dec * len_s
    assert len_p % A_CHUNK == 0 and len_s <= A_CHUNK and (len_s & (len_s - 1)) == 0

    def row(v):
        return v.reshape(1, -1)

    def to_tm(v):
        return jnp.swapaxes(v, 0, 1).reshape((t_s,) + v.shape[2:])

    def from_tm(v):
        return jnp.swapaxes(v.reshape((len_s, n_dec) + v.shape[1:]), 0, 1)

    hp = x_prompt.reshape(n_pr * len_p, D_MODEL)
    hs = to_tm(x_sample)

    cos_p, sin_p = _rope_tables(jnp.arange(len_p, dtype=jnp.int32))
    pos_s = past_len + jnp.arange(len_s, dtype=jnp.int32)
    cos_s, sin_s = _rope_tables(jnp.repeat(pos_s, n_dec))

    tril = jnp.tril(jnp.ones((A_CHUNK, A_CHUNK), dtype=bool))
    eye_req = jnp.eye(n_dec, dtype=F32)

    chunk_v_s, mla_p, mla_s = [], [], []
    lru_h_p, lru_h_s, lru_c_p, lru_c_s = [], [], [], []
    for layer in range(depth):
        kind, j = layer % 3, layer // 3
        g_mix = row(norm_mix[layer])
        if kind == 0:
            w_causal = jnp.where(tril[None], a_w_s[j], 0.0)
            w_small = w_causal[:, :len_s, :len_s]
            wmix_s = jnp.einsum("gts,rq->gtrsq", w_small, eye_req).reshape(A_GROUPS, t_s, t_s)
            pad = A_CHUNK - t_s
            wmix_s = jnp.pad(wmix_s, ((0, 0), (0, pad), (0, pad)))
            bmix_p = a_b_s[j].T
            bmix_s = jnp.pad(jnp.repeat(a_b_s[j].T[:len_s], n_dec, axis=0), ((0, pad), (0, 0)))
            args = (a_w_in[j].astype(BF), row(a_ln_g[j]), row(a_ln_b[j]))
            w_out = a_w_out[j].astype(BF)
            hp, _ = _mixer_a(hp, g_mix, *args, w_causal.astype(BF), bmix_p, w_out, tm=512)
            hs, v_s = _mixer_a(hs, g_mix, *args, wmix_s.astype(BF), bmix_s, w_out, tm=t_s)
            chunk_v_s.append(from_tm(v_s))
        elif kind == 1:
            nope_cols = (jnp.arange(MLA_HEADS)[:, None] * (MLA_NOPE + MLA_ROPE)
                         + jnp.arange(MLA_NOPE)[None, :]).reshape(-1)
            pe_cols = (jnp.arange(MLA_HEADS)[:, None] * (MLA_NOPE + MLA_ROPE) + MLA_NOPE
                       + jnp.arange(MLA_ROPE)[None, :])
            pe_cols_sw = _swap_halves(pe_cols)
            w = dict(
                wdq=b_w_dq[j].astype(BF), qn=row(b_q_norm[j]),
                wuqn=b_w_uq[j][:, nope_cols].astype(BF),
                wuqa=b_w_uq[j][:, pe_cols.reshape(-1)].astype(BF),
                wuqb=b_w_uq[j][:, pe_cols_sw.reshape(-1)].astype(BF),
                wdkv=b_w_dkv[j].astype(BF),
                wdkvb=_swap_halves(b_w_dkv[j][:, MLA_KV_LORA:]).astype(BF),
                kvn=row(b_kv_norm[j]),
                wuk=jnp.transpose(b_w_uk[j], (1, 2, 0)).astype(BF),
            )
            wuv = jnp.transpose(b_w_uv[j], (1, 0, 2)).astype(BF)
            wout = b_w_out[j].astype(BF)

            tq = 256
            kv_p, qlat_p, qpe_p = _mla_proj(hp, g_mix, cos_p, sin_p, w, tq=tq)
            o_p = _mla_prompt_attn(qlat_p, qpe_p, kv_p, n_batch=n_pr, seq=len_p, tq=tq, tk=512)
            hp = _mla_out(hp, o_p, wuv, wout, tq=tq)
            mla_p.append(kv_p.reshape(n_pr, len_p, MLA_LAT))

            kv_s, qlat_s, qpe_s = _mla_proj(hs, g_mix, cos_s, sin_s, w, tq=t_s)

            def per_request(q):
                q = q.reshape(MLA_HEADS, len_s, n_dec, q.shape[-1])
                return jnp.transpose(q, (2, 0, 1, 3)).reshape(n_dec, MLA_HEADS * len_s, q.shape[-1])

            kv_new = jnp.pad(from_tm(kv_s), ((0, 0), (0, 16 - len_s), (0, 0)))
            o_s = _mla_decode_attn(page_table, per_request(qlat_s), per_request(qpe_s), kv_new,
                                   jnp.swapaxes(cache_mla, 2, 3), layer=j, n_new=len_s,
                                   pages_per_step=32)
            o_s = o_s.reshape(n_dec, MLA_HEADS, len_s, MLA_KV_LORA)
            o_s = jnp.transpose(o_s, (1, 2, 0, 3)).reshape(1, MLA_HEADS * t_s, MLA_KV_LORA)
            hs = _mla_out(hs, o_s, wuv, wout, tq=t_s)
            mla_s.append(from_tm(kv_s))
        else:
            w = dict(
                wx=c_w_x[j].astype(BF), wgate=c_w_gate[j].astype(BF), convw=c_conv_w[j],
                convb=row(c_conv_b[j]), wa=c_w_a[j].astype(BF), ba=row(c_b_a[j]),
                wi=c_w_i[j].astype(BF), bi=row(c_b_i[j]), lam=row(c_lambda[j]),
                wout=c_w_out[j].astype(BF),
            )
            hp, hl_p, cb_p = _mixer_c_prompt(hp, g_mix, w, n_batch=n_pr, seq=len_p, tt=256)
            lru_h_p.append(hl_p.reshape(n_pr, LRU_WIDTH))
            lru_c_p.append(cb_p)
            buf0 = jnp.swapaxes(state_lru_conv[j], 0, 1).reshape(-1, LRU_WIDTH)
            hs, hl_s, cb_s = _mixer_c_sample(hs, g_mix, state_lru_h[j], buf0, w, n_req=n_dec)
            lru_h_s.append(hl_s)
            lru_c_s.append(jnp.swapaxes(cb_s.reshape(CONV_WIDTH - 1, n_dec, LRU_WIDTH), 0, 1))

        last = layer == depth - 1
        ffn_w = (row(norm_ffn[layer]), f_w_gate, f_w_up, f_w_down, row(norm_out))
        hp = _ffn(hp, *ffn_w, layer=layer, tm=1024, tf=256, final_norm=last)
        hs = _ffn(hs, *ffn_w, layer=layer, tm=t_s, tf=256, final_norm=last)

    return (hp.reshape(n_pr, len_p, D_MODEL), from_tm(hs), jnp.stack(chunk_v_s),
            jnp.stack(mla_p), jnp.stack(mla_s), jnp.stack(lru_h_p), jnp.stack(lru_h_s),
            jnp.stack(lru_c_p), jnp.stack(lru_c_s))
```

```python
import functools

import jax
import jax.numpy as jnp
import numpy as np
from jax import lax
from jax.experimental import pallas as pl
from jax.experimental.pallas import tpu as pltpu

BF = jnp.bfloat16
F32 = jnp.float32

D_MODEL = 1024
D_FF = 2816
A_CHUNK = 128
A_GROUPS = 8
MLA_HEADS = 8
MLA_Q_LORA = 384
MLA_KV_LORA = 256
MLA_NOPE = 128
MLA_ROPE = 64
MLA_V = 128
MLA_LAT = MLA_KV_LORA + MLA_ROPE
MLA_SCALE = (MLA_NOPE + MLA_ROPE) ** -0.5
ROPE_THETA = 10000.0
PAGE_SIZE = 128
LRU_WIDTH = 1280
LRU_BLOCKS = 10
LRU_BLOCK_DIM = 128
CONV_WIDTH = 4
LRU_C = 8.0
RMS_EPS = 1e-6
LN_EPS = 1e-5

SUBLANES = 8
LANES = 128
LOG2E = 1.4426950408889634
VMEM_LIMIT = 56 * 1024 * 1024


def _params(*sem):
    return pltpu.CompilerParams(dimension_semantics=sem, vmem_limit_bytes=VMEM_LIMIT)


def _dot(a, b):
    return jnp.dot(a.astype(BF), b.astype(BF), preferred_element_type=F32)


def _dot_nt(a, b):
    return lax.dot_general(a.astype(BF), b.astype(BF), (((1,), (1,)), ((), ())),
                           preferred_element_type=F32)


def _rmsnorm(x, g):
    return x * lax.rsqrt(jnp.mean(x * x, axis=-1, keepdims=True) + RMS_EPS) * g


def _layernorm(x, g, b):
    mu = jnp.mean(x, axis=-1, keepdims=True)
    xc = x - mu
    return xc * lax.rsqrt(jnp.mean(xc * xc, axis=-1, keepdims=True) + LN_EPS) * g + b


def _sigmoid(x):
    return 0.5 * jnp.tanh(0.5 * x) + 0.5


def _full(shape):
    return pl.BlockSpec(shape, lambda *_: (0,) * len(shape))


def _ffn_kernel(x_ref, g_ref, wg_ref, wu_ref, wd_ref, gout_ref, o_ref, h_scr, *,
                final_norm, n_split):
    x = x_ref[...]
    xn = _rmsnorm(x, g_ref[...]).astype(BF)
    width = D_FF // n_split
    for c in range(n_split):
        cols = slice(c * width, (c + 1) * width)
        gate = _dot(xn, wg_ref[:, cols])
        up = _dot(xn, wu_ref[:, cols])
        h_scr[:, cols] = (gate * _sigmoid(gate) * up).astype(BF)
    y = x + _dot(h_scr[...], wd_ref[...])
    if final_norm:
        y = _rmsnorm(y, gout_ref[...])
    o_ref[...] = y


def _ffn(h, g, wg, wu, wd, gout, *, layer, tm, final_norm):
    t = h.shape[0]

    def resident(shape):
        return pl.BlockSpec((None,) + shape, lambda i: (layer, 0, 0),
                            pipeline_mode=pl.Buffered(1))

    return pl.pallas_call(
        functools.partial(_ffn_kernel, final_norm=final_norm, n_split=2),
        grid=(t // tm,),
        in_specs=[
            pl.BlockSpec((tm, D_MODEL), lambda i: (i, 0)),
            _full((1, D_MODEL)),
            resident((D_MODEL, D_FF)),
            resident((D_MODEL, D_FF)),
            resident((D_FF, D_MODEL)),
            _full((1, D_MODEL)),
        ],
        out_specs=pl.BlockSpec((tm, D_MODEL), lambda i: (i, 0)),
        out_shape=jax.ShapeDtypeStruct((t, D_MODEL), F32),
        scratch_shapes=[pltpu.VMEM((tm, D_FF), BF)],
        compiler_params=_params("parallel"),
        name="ffn",
    )(h, g, wg, wu, wd, gout)


def _mixer_a_kernel(x_ref, g_ref, win_ref, lng_ref, lnb_ref, wmix_ref, bmix_ref, wout_ref,
                    o_ref, *rest, emit_v):
    u_scr, vb_scr, uv_scr = rest[-3:]
    tm = x_ref.shape[0]
    x = x_ref[...]
    xn = _rmsnorm(x, g_ref[...]).astype(BF)
    u_scr[...] = jax.nn.gelu(_dot(xn, win_ref[:, :D_MODEL]))
    v = _layernorm(jax.nn.gelu(_dot(xn, win_ref[:, D_MODEL:])), lng_ref[...], lnb_ref[...])
    if emit_v:
        rest[0][...] = v
    vb_scr[...] = v.astype(BF)
    for c in range(tm // A_CHUNK):
        rows = slice(c * A_CHUNK, (c + 1) * A_CHUNK)
        for g in range(A_GROUPS):
            cols = slice(g * A_CHUNK, (g + 1) * A_CHUNK)
            mixed = _dot(wmix_ref[g], vb_scr[rows, cols]) + bmix_ref[:, g:g + 1]
            uv_scr[rows, cols] = (u_scr[rows, cols] * mixed).astype(BF)
    o_ref[...] = x + _dot(uv_scr[...], wout_ref[...])


def _mixer_a(h, g, win, lng, lnb, wmix, bmix, wout, *, tm, emit_v):
    t = h.shape[0]
    n_out = 2 if emit_v else 1
    return pl.pallas_call(
        functools.partial(_mixer_a_kernel, emit_v=emit_v),
        grid=(t // tm,),
        in_specs=[
            pl.BlockSpec((tm, D_MODEL), lambda i: (i, 0)),
            _full((1, D_MODEL)),
            _full((D_MODEL, 2 * D_MODEL)),
            _full((1, D_MODEL)),
            _full((1, D_MODEL)),
            _full((A_GROUPS, A_CHUNK, A_CHUNK)),
            _full((A_CHUNK, A_GROUPS)),
            _full((D_MODEL, D_MODEL)),
        ],
        out_specs=[pl.BlockSpec((tm, D_MODEL), lambda i: (i, 0))] * n_out,
        out_shape=[jax.ShapeDtypeStruct((t, D_MODEL), F32)] * n_out,
        scratch_shapes=[pltpu.VMEM((tm, D_MODEL), F32), pltpu.VMEM((tm, D_MODEL), BF),
                        pltpu.VMEM((tm, D_MODEL), BF)],
        compiler_params=_params("parallel"),
        name="mixer_a",
    )(h, g, win, lng, lnb, wmix, bmix, wout)


def _mla_proj_kernel(x_ref, g_ref, cos_ref, sin_ref, wdq_ref, qn_ref, wuqn_ref, wuqa_ref,
                     wuqb_ref, wdkv_ref, wdkvb_ref, kvn_ref, wuk_ref,
                     kv_ref, qlat_ref, qpe_ref):
    tq = x_ref.shape[0]
    xn = _rmsnorm(x_ref[...], g_ref[...]).astype(BF)
    cq = _rmsnorm(_dot(xn, wdq_ref[...]), qn_ref[...]).astype(BF)
    q_nope = _dot(cq, wuqn_ref[...])
    cos = cos_ref[...]
    sin = sin_ref[...]
    q_pe = _dot(cq, wuqa_ref[...]) * cos + _dot(cq, wuqb_ref[...]) * sin
    kv = _dot(xn, wdkv_ref[...])
    kv_ref[:, :MLA_KV_LORA] = _rmsnorm(kv[:, :MLA_KV_LORA], kvn_ref[...])
    kv_ref[:, MLA_KV_LORA:] = (kv[:, MLA_KV_LORA:] * cos[:, :MLA_ROPE]
                               + _dot(xn, wdkvb_ref[...]) * sin[:, :MLA_ROPE])
    for h in range(MLA_HEADS):
        rows = slice(h * tq, (h + 1) * tq)
        qlat_ref[0, rows, :] = _dot(q_nope[:, h * MLA_NOPE:(h + 1) * MLA_NOPE],
                                    wuk_ref[h]).astype(BF)
        qpe_ref[0, rows, :] = q_pe[:, h * MLA_ROPE:(h + 1) * MLA_ROPE].astype(BF)


def _mla_proj(h, g, cos, sin, w, *, tq):
    t = h.shape[0]
    nb = t // tq
    npos = cos.shape[0] // tq
    hq = MLA_HEADS * tq
    return pl.pallas_call(
        _mla_proj_kernel,
        grid=(nb,),
        in_specs=[
            pl.BlockSpec((tq, D_MODEL), lambda i: (i, 0)),
            _full((1, D_MODEL)),
            pl.BlockSpec((tq, MLA_HEADS * MLA_ROPE), lambda i: (i % npos, 0)),
            pl.BlockSpec((tq, MLA_HEADS * MLA_ROPE), lambda i: (i % npos, 0)),
            _full((D_MODEL, MLA_Q_LORA)),
            _full((1, MLA_Q_LORA)),
            _full((MLA_Q_LORA, MLA_HEADS * MLA_NOPE)),
            _full((MLA_Q_LORA, MLA_HEADS * MLA_ROPE)),
            _full((MLA_Q_LORA, MLA_HEADS * MLA_ROPE)),
            _full((D_MODEL, MLA_LAT)),
            _full((D_MODEL, MLA_ROPE)),
            _full((1, MLA_KV_LORA)),
            _full((MLA_HEADS, MLA_NOPE, MLA_KV_LORA)),
        ],
        out_specs=[pl.BlockSpec((tq, MLA_LAT), lambda i: (i, 0)),
                   pl.BlockSpec((1, hq, MLA_KV_LORA), lambda i: (i, 0, 0)),
                   pl.BlockSpec((1, hq, MLA_ROPE), lambda i: (i, 0, 0))],
        out_shape=[jax.ShapeDtypeStruct((t, MLA_LAT), F32),
                   jax.ShapeDtypeStruct((nb, hq, MLA_KV_LORA), BF),
                   jax.ShapeDtypeStruct((nb, hq, MLA_ROPE), BF)],
        compiler_params=_params("parallel"),
        name="mla_proj",
    )(h, g, cos, sin, w["wdq"], w["qn"], w["wuqn"], w["wuqa"], w["wuqb"], w["wdkv"],
      w["wdkvb"], w["kvn"], w["wuk"])


def _lanes(x, n):
    return x[:, :n] if n <= LANES else jnp.concatenate([x] * (n // LANES), axis=-1)


def _softmax_step(s, pv, m_scr, l_scr, acc_scr, rows):
    c2 = MLA_SCALE * LOG2E
    m_prev = m_scr[rows, :]
    m_new = jnp.maximum(m_prev, jnp.max(s, axis=-1, keepdims=True))
    alpha = jnp.exp2((m_prev - m_new) * c2)
    p = jnp.exp2((s - _lanes(m_new, s.shape[-1])) * c2)
    l_scr[rows, :] = alpha * l_scr[rows, :] + jnp.sum(p, axis=-1, keepdims=True)
    acc_scr[rows, :] = _lanes(alpha, MLA_KV_LORA) * acc_scr[rows, :] + pv(p)
    m_scr[rows, :] = m_new


def _mla_value_out(o_head, x, wuv_ref, wout_ref, ov_scr):
    for h in range(MLA_HEADS):
        ov_scr[:, h * MLA_V:(h + 1) * MLA_V] = _dot(o_head(h), wuv_ref[h]).astype(BF)
    return x + _dot(ov_scr[...], wout_ref[...])


def _mla_prompt_attn_kernel(qi_ref, kj_ref, qlat_ref, qpe_ref, kv_ref, x_ref, wuv_ref, wout_ref,
                            y_ref, m_scr, l_scr, acc_scr, ov_scr, *, tq, tk, rb):
    step_id = pl.program_id(1)
    i = qi_ref[step_id]
    j = kj_ref[step_id]
    j_diag = ((i + 1) * tq - 1) // tk

    @pl.when(j == 0)
    def _():
        m_scr[...] = jnp.full_like(m_scr, -jnp.inf)
        l_scr[...] = jnp.zeros_like(l_scr)
        acc_scr[...] = jnp.zeros_like(acc_scr)

    def step(masked):
        kb = kv_ref[...].astype(BF)
        klat, kpe = kb[:, :MLA_KV_LORA], kb[:, MLA_KV_LORA:]
        for r in range(MLA_HEADS * tq // rb):
            rows = slice(r * rb, (r + 1) * rb)
            s = _dot_nt(qlat_ref[0, rows, :], klat) + _dot_nt(qpe_ref[0, rows, :], kpe)
            if masked:
                tok = lax.broadcasted_iota(jnp.int32, (rb, tk), 0) & (tq - 1)
                k_pos = j * tk + lax.broadcasted_iota(jnp.int32, (rb, tk), 1)
                s = jnp.where(k_pos <= i * tq + tok, s, -jnp.inf)
            _softmax_step(s, lambda p: _dot(p, klat), m_scr, l_scr, acc_scr, rows)

    @pl.when(j < j_diag)
    def _():
        step(False)

    @pl.when(j == j_diag)
    def _():
        step(True)

        def o_head(h):
            rows = slice(h * tq, (h + 1) * tq)
            return acc_scr[rows, :] / _lanes(l_scr[rows, :], MLA_KV_LORA)

        y_ref[...] = _mla_value_out(o_head, x_ref[...], wuv_ref, wout_ref, ov_scr)


def _causal_schedule(nq, tq, tk):
    pairs = [(i, j) for i in range(nq) for j in range(((i + 1) * tq - 1) // tk + 1)]
    qi, kj = zip(*pairs)
    return np.asarray(qi, np.int32), np.asarray(kj, np.int32)


def _mla_prompt_attn(qlat, qpe, kv, h, wuv, wout, *, n_batch, seq, tq, tk):
    nq, nk = seq // tq, seq // tk
    hq = MLA_HEADS * tq
    qi, kj = _causal_schedule(nq, tq, tk)

    def q_map(b, s, qi_ref, kj_ref):
        return (b * nq + qi_ref[s], 0, 0)

    def tok_map(b, s, qi_ref, kj_ref):
        return (b * nq + qi_ref[s], 0)

    grid_spec = pltpu.PrefetchScalarGridSpec(
        num_scalar_prefetch=2,
        grid=(n_batch, len(qi)),
        in_specs=[
            pl.BlockSpec((1, hq, MLA_KV_LORA), q_map),
            pl.BlockSpec((1, hq, MLA_ROPE), q_map),
            pl.BlockSpec((tk, MLA_LAT), lambda b, s, qi_ref, kj_ref: (b * nk + kj_ref[s], 0)),
            pl.BlockSpec((tq, D_MODEL), tok_map),
            _full((MLA_HEADS, MLA_KV_LORA, MLA_V)),
            _full((MLA_HEADS * MLA_V, D_MODEL)),
        ],
        out_specs=pl.BlockSpec((tq, D_MODEL), tok_map),
        scratch_shapes=[pltpu.VMEM((hq, LANES), F32), pltpu.VMEM((hq, LANES), F32),
                        pltpu.VMEM((hq, MLA_KV_LORA), F32),
                        pltpu.VMEM((tq, MLA_HEADS * MLA_V), BF)],
    )
    return pl.pallas_call(
        functools.partial(_mla_prompt_attn_kernel, tq=tq, tk=tk, rb=512),
        grid_spec=grid_spec,
        out_shape=jax.ShapeDtypeStruct((n_batch * seq, D_MODEL), F32),
        compiler_params=_params("parallel", "arbitrary"),
        name="mla_prompt_attn",
    )(qi, kj, qlat, qpe, kv, h, wuv, wout)


def _mla_decode_attn_kernel(pt_ref, qlat_ref, qpe_ref, kvnew_ref, *rest, n_pages, n_new):
    del pt_ref
    page_refs = rest[:n_pages]
    o_ref, kbuf, m_scr, l_scr, acc_scr = rest[n_pages:]
    s_id = pl.program_id(1)
    qlat = qlat_ref[0]
    qpe = qpe_ref[0]
    every_row = slice(None)

    @pl.when(s_id == 0)
    def _():
        m_scr[...] = jnp.full_like(m_scr, -jnp.inf)
        l_scr[...] = jnp.zeros_like(l_scr)
        acc_scr[...] = jnp.zeros_like(acc_scr)
        kn = kvnew_ref[0].astype(BF)
        klat, kpe = kn[:, :MLA_KV_LORA], kn[:, MLA_KV_LORA:]
        s = _dot_nt(qlat, klat) + _dot_nt(qpe, kpe)
        tok = lax.broadcasted_iota(jnp.int32, s.shape, 0) & (n_new - 1)
        col = lax.broadcasted_iota(jnp.int32, s.shape, 1)
        s = jnp.where(col <= tok, s, -jnp.inf)
        _softmax_step(s, lambda p: _dot(p, klat), m_scr, l_scr, acc_scr, every_row)

    for p_id in range(n_pages):
        kbuf[:, p_id * PAGE_SIZE:(p_id + 1) * PAGE_SIZE] = page_refs[p_id][...].astype(BF)
    klat_t = kbuf[:MLA_KV_LORA, :]
    s = _dot(qlat, klat_t) + _dot(qpe, kbuf[MLA_KV_LORA:, :])
    _softmax_step(s, lambda p: _dot_nt(p, klat_t), m_scr, l_scr, acc_scr, every_row)

    @pl.when(s_id == pl.num_programs(1) - 1)
    def _():
        o_ref[0] = acc_scr[...] / _lanes(l_scr[...], MLA_KV_LORA)


def _mla_decode_attn(page_table, qlat, qpe, kvnew, cache_t, *, layer, n_new, pages_per_step):
    n_req, n_pages_total = page_table.shape
    rows = qlat.shape[1]
    steps = n_pages_total // pages_per_step

    def page_map(p_id):
        return lambda r, s, pt: (layer, pt[r, s * pages_per_step + p_id], 0, 0)

    grid_spec = pltpu.PrefetchScalarGridSpec(
        num_scalar_prefetch=1,
        grid=(n_req, steps),
        in_specs=[
            pl.BlockSpec((1, rows, MLA_KV_LORA), lambda r, s, pt: (r, 0, 0)),
            pl.BlockSpec((1, rows, MLA_ROPE), lambda r, s, pt: (r, 0, 0)),
            pl.BlockSpec((1,) + kvnew.shape[1:], lambda r, s, pt: (r, 0, 0)),
        ] + [pl.BlockSpec((None, None, MLA_LAT, PAGE_SIZE), page_map(p_id))
             for p_id in range(pages_per_step)],
        out_specs=pl.BlockSpec((1, rows, MLA_KV_LORA), lambda r, s, pt: (r, 0, 0)),
        scratch_shapes=[pltpu.VMEM((MLA_LAT, pages_per_step * PAGE_SIZE), BF),
                        pltpu.VMEM((rows, LANES), F32), pltpu.VMEM((rows, LANES), F32),
                        pltpu.VMEM((rows, MLA_KV_LORA), F32)],
    )
    return pl.pallas_call(
        functools.partial(_mla_decode_attn_kernel, n_pages=pages_per_step, n_new=n_new),
        grid_spec=grid_spec,
        out_shape=jax.ShapeDtypeStruct((n_req, rows, MLA_KV_LORA), F32),
        compiler_params=_params("parallel", "arbitrary"),
        name="mla_decode_attn",
    )(page_table, qlat, qpe, kvnew, *([cache_t] * pages_per_step))


def _mla_out_kernel(x_ref, o_ref, wuv_ref, wout_ref, y_ref, ov_scr):
    tq = x_ref.shape[0]
    y_ref[...] = _mla_value_out(lambda h: o_ref[0, h * tq:(h + 1) * tq, :], x_ref[...],
                                wuv_ref, wout_ref, ov_scr)


def _mla_out(h, o, wuv, wout, *, tq):
    t = h.shape[0]
    hq = MLA_HEADS * tq
    return pl.pallas_call(
        _mla_out_kernel,
        grid=(t // tq,),
        in_specs=[
            pl.BlockSpec((tq, D_MODEL), lambda i: (i, 0)),
            pl.BlockSpec((1, hq, MLA_KV_LORA), lambda i: (i, 0, 0)),
            _full((MLA_HEADS, MLA_KV_LORA, MLA_V)),
            _full((MLA_HEADS * MLA_V, D_MODEL)),
        ],
        out_specs=pl.BlockSpec((tq, D_MODEL), lambda i: (i, 0)),
        out_shape=jax.ShapeDtypeStruct((t, D_MODEL), F32),
        scratch_shapes=[pltpu.VMEM((tq, MLA_HEADS * MLA_V), BF)],
        compiler_params=_params("parallel"),
        name="mla_out",
    )(h, o, wuv, wout)


def _block_diag(x, w_ref, b):
    parts = [_dot(x[:, n * LRU_BLOCK_DIM:(n + 1) * LRU_BLOCK_DIM], w_ref[n])
             for n in range(LRU_BLOCKS)]
    return jnp.concatenate(parts, axis=-1) + b


def _lru_coeffs(xc, wa_ref, ba_ref, wi_ref, bi_ref, lam_ref):
    r = _sigmoid(_block_diag(xc, wa_ref, ba_ref[...]))
    gi = _sigmoid(_block_diag(xc, wi_ref, bi_ref[...]))
    neg_lam = -lam_ref[...]
    softplus = jnp.maximum(neg_lam, 0.0) + jnp.log1p(jnp.exp(-jnp.abs(neg_lam)))
    a = jnp.exp((-LRU_C * softplus) * r)
    return a, jnp.sqrt(1.0 - a * a) * gi * xc


def _scan_pitch(seg):
    pitch = seg + SUBLANES
    return pitch if (pitch // SUBLANES) % 2 else pitch + SUBLANES


def _mixer_c_prompt_kernel(x_ref, g_ref, wx_ref, wgate_ref, convw_ref, convb_ref, wa_ref, ba_ref,
                           wi_ref, bi_ref, lam_ref, wout_ref,
                           o_ref, hlast_ref, buf_ref,
                           xx_scr, gate_scr, a_scr, b_scr, hloc_scr, ploc_scr, carry_scr):
    tt = x_ref.shape[0]
    seg = tt // SUBLANES
    pitch = _scan_pitch(seg)
    ti = pl.program_id(1)

    @pl.when(ti == 0)
    def _():
        xx_scr[0:SUBLANES, :] = jnp.zeros((SUBLANES, LRU_WIDTH), F32)
        carry_scr[...] = jnp.zeros_like(carry_scr)

    x = x_ref[...]
    xn = _rmsnorm(x, g_ref[...]).astype(BF)
    gate_scr[...] = jax.nn.gelu(_dot(xn, wgate_ref[...]))
    xx_scr[SUBLANES:, :] = _dot(xn, wx_ref[...])
    xc = convb_ref[...]
    for k in range(CONV_WIDTH):
        off = SUBLANES - (CONV_WIDTH - 1) + k
        xc = xc + convw_ref[k:k + 1, :] * xx_scr[off:off + tt, :]
    buf_ref[0] = xx_scr[tt + SUBLANES - (CONV_WIDTH - 1):, :]
    xx_scr[0:SUBLANES, :] = xx_scr[tt:tt + SUBLANES, :]

    a, b = _lru_coeffs(xc, wa_ref, ba_ref, wi_ref, bi_ref, lam_ref)
    for n in range(LRU_BLOCKS):
        cols = slice(n * LRU_BLOCK_DIM, (n + 1) * LRU_BLOCK_DIM)
        for k in range(SUBLANES):
            a_scr[n, k * pitch:k * pitch + seg, :] = a[k * seg:(k + 1) * seg, cols]
            b_scr[n, k * pitch:k * pitch + seg, :] = b[k * seg:(k + 1) * seg, cols]

    def body(j, carry):
        hs, ps = carry
        rows = pl.ds(j, SUBLANES, stride=pitch)
        new_h, new_p = [], []
        for n in range(LRU_BLOCKS):
            aj = a_scr[n, rows, :]
            h = aj * hs[n] + b_scr[n, rows, :]
            p = aj * ps[n]
            hloc_scr[n, rows, :] = h
            ploc_scr[n, rows, :] = p
            new_h.append(h)
            new_p.append(p)
        return tuple(new_h), tuple(new_p)

    h_end, p_end = lax.fori_loop(
        0, seg, body,
        ((jnp.zeros((SUBLANES, LRU_BLOCK_DIM), F32),) * LRU_BLOCKS,
         (jnp.ones((SUBLANES, LRU_BLOCK_DIM), F32),) * LRU_BLOCKS))

    for n in range(LRU_BLOCKS):
        cols = slice(n * LRU_BLOCK_DIM, (n + 1) * LRU_BLOCK_DIM)
        c = carry_scr[:, cols]
        for k in range(SUBLANES):
            rows = slice(k * seg, (k + 1) * seg)
            srows = slice(k * pitch, k * pitch + seg)
            gate_scr[rows, cols] = gate_scr[rows, cols] * (hloc_scr[n, srows, :]
                                                           + ploc_scr[n, srows, :] * c)
            c = p_end[n][k:k + 1, :] * c + h_end[n][k:k + 1, :]
        carry_scr[:, cols] = c
    hlast_ref[0] = carry_scr[...]
    o_ref[...] = x + _dot(gate_scr[...], wout_ref[...])


def _mixer_c_prompt(h, g, w, *, n_batch, seq, tt):
    nt = seq // tt
    wd = LRU_WIDTH
    blk = (LRU_BLOCKS, LRU_BLOCK_DIM, LRU_BLOCK_DIM)
    return pl.pallas_call(
        _mixer_c_prompt_kernel,
        grid=(n_batch, nt),
        in_specs=[
            pl.BlockSpec((tt, D_MODEL), lambda b, i: (b * nt + i, 0)),
            _full((1, D_MODEL)),
            _full((D_MODEL, wd)), _full((D_MODEL, wd)),
            _full((CONV_WIDTH, wd)), _full((1, wd)),
            _full(blk), _full((1, wd)), _full(blk), _full((1, wd)), _full((1, wd)),
            _full((wd, D_MODEL)),
        ],
        out_specs=[pl.BlockSpec((tt, D_MODEL), lambda b, i: (b * nt + i, 0)),
                   pl.BlockSpec((1, 1, wd), lambda b, i: (b, 0, 0)),
                   pl.BlockSpec((1, CONV_WIDTH - 1, wd), lambda b, i: (b, 0, 0))],
        out_shape=[jax.ShapeDtypeStruct((n_batch * seq, D_MODEL), F32),
                   jax.ShapeDtypeStruct((n_batch, 1, wd), F32),
                   jax.ShapeDtypeStruct((n_batch, CONV_WIDTH - 1, wd), F32)],
        scratch_shapes=[pltpu.VMEM((tt + SUBLANES, wd), F32), pltpu.VMEM((tt, wd), F32)] +
                       [pltpu.VMEM((LRU_BLOCKS, SUBLANES * _scan_pitch(tt // SUBLANES),
                                    LRU_BLOCK_DIM), F32)] * 4 +
                       [pltpu.VMEM((1, wd), F32)],
        compiler_params=_params("parallel", "arbitrary"),
        name="mixer_c_prompt",
    )(h, g, w["wx"], w["wgate"], w["convw"], w["convb"], w["wa"], w["ba"], w["wi"], w["bi"],
      w["lam"], w["wout"])


def _mixer_c_sample_kernel(x_ref, g_ref, h0_ref, buf0_ref, wx_ref, wgate_ref, convw_ref, convb_ref,
                           wa_ref, ba_ref, wi_ref, bi_ref, lam_ref, wout_ref,
                           o_ref, hlast_ref, buf_ref, xx_scr, gh_scr, *, n_req):
    t = x_ref.shape[0]
    hist = (CONV_WIDTH - 1) * n_req
    x = x_ref[...]
    xn = _rmsnorm(x, g_ref[...]).astype(BF)
    gate = jax.nn.gelu(_dot(xn, wgate_ref[...]))
    xx_scr[0:hist, :] = buf0_ref[...]
    xx_scr[hist:, :] = _dot(xn, wx_ref[...])
    xc = convb_ref[...]
    for k in range(CONV_WIDTH):
        xc = xc + convw_ref[k:k + 1, :] * xx_scr[k * n_req:k * n_req + t, :]
    buf_ref[...] = xx_scr[t:, :]
    a, b = _lru_coeffs(xc, wa_ref, ba_ref, wi_ref, bi_ref, lam_ref)
    h = h0_ref[...]
    for step in range(t // n_req):
        rows = slice(step * n_req, (step + 1) * n_req)
        h = a[rows, :] * h + b[rows, :]
        gh_scr[rows, :] = gate[rows, :] * h
    hlast_ref[...] = h
    o_ref[...] = x + _dot(gh_scr[...], wout_ref[...])


def _mixer_c_sample(h, g, h0, buf0, w, *, n_req):
    t = h.shape[0]
    wd = LRU_WIDTH
    hist = (CONV_WIDTH - 1) * n_req
    blk = (LRU_BLOCKS, LRU_BLOCK_DIM, LRU_BLOCK_DIM)
    return pl.pallas_call(
        functools.partial(_mixer_c_sample_kernel, n_req=n_req),
        grid=(1,),
        in_specs=[
            _full((t, D_MODEL)), _full((1, D_MODEL)), _full((n_req, wd)), _full((hist, wd)),
            _full((D_MODEL, wd)), _full((D_MODEL, wd)),
            _full((CONV_WIDTH, wd)), _full((1, wd)),
            _full(blk), _full((1, wd)), _full(blk), _full((1, wd)), _full((1, wd)),
            _full((wd, D_MODEL)),
        ],
        out_specs=[_full((t, D_MODEL)), _full((n_req, wd)), _full((hist, wd))],
        out_shape=[jax.ShapeDtypeStruct((t, D_MODEL), F32),
                   jax.ShapeDtypeStruct((n_req, wd), F32),
                   jax.ShapeDtypeStruct((hist, wd), F32)],
        scratch_shapes=[pltpu.VMEM((hist + t, wd), F32), pltpu.VMEM((t, wd), F32)],
        compiler_params=_params("arbitrary"),
        name="mixer_c_sample",
    )(h, g, h0, buf0, w["wx"], w["wgate"], w["convw"], w["convb"], w["wa"], w["ba"], w["wi"],
      w["bi"], w["lam"], w["wout"])


def _rope_tables(pos):
    inv = 1.0 / (ROPE_THETA ** (jnp.arange(0, MLA_ROPE, 2, dtype=F32) / MLA_ROPE))
    ang = pos.astype(F32)[:, None] * inv[None, :]
    cos, sin = jnp.cos(ang), jnp.sin(ang)
    cos_t = jnp.tile(jnp.concatenate([cos, cos], axis=-1), (1, MLA_HEADS))
    sin_t = jnp.tile(jnp.concatenate([-sin, sin], axis=-1), (1, MLA_HEADS))
    return cos_t, sin_t


def _swap_halves(w):
    half = w.shape[-1] // 2
    return jnp.concatenate([w[..., half:], w[..., :half]], axis=-1)


def kernel(x_prompt, x_sample, cache_mla, state_lru_h, state_lru_conv, page_table, norm_mix, norm_ffn, norm_out, a_w_in, a_ln_g, a_ln_b, a_w_s, a_b_s, a_w_out, b_w_dq, b_q_norm, b_w_uq, b_w_dkv, b_kv_norm, b_w_uk, b_w_uv, b_w_out, c_w_x, c_w_gate, c_conv_w, c_conv_b, c_w_a, c_b_a, c_w_i, c_b_i, c_lambda, c_w_out, f_w_gate, f_w_up, f_w_down):
    n_pr, len_p, _ = x_prompt.shape
    n_dec, len_s, _ = x_sample.shape
    depth = norm_mix.shape[0]
    past_len = page_table.shape[1] * PAGE_SIZE
    t_s = n_dec * len_s
    assert len_p % A_CHUNK == 0 and len_s <= A_CHUNK and (len_s & (len_s - 1)) == 0

    def row(v):
        return v.reshape(1, -1)

    def to_tm(v):
        return jnp.swapaxes(v, 0, 1).reshape((t_s,) + v.shape[2:])

    def from_tm(v):
        return jnp.swapaxes(v.reshape((len_s, n_dec) + v.shape[1:]), 0, 1)

    hp = x_prompt.reshape(n_pr * len_p, D_MODEL)
    hs = to_tm(x_sample)

    cos_p, sin_p = _rope_tables(jnp.arange(len_p, dtype=jnp.int32))
    pos_s = past_len + jnp.arange(len_s, dtype=jnp.int32)
    cos_s, sin_s = _rope_tables(jnp.repeat(pos_s, n_dec))

    f_wg, f_wu, f_wd = f_w_gate.astype(BF), f_w_up.astype(BF), f_w_down.astype(BF)
    tril = jnp.tril(jnp.ones((A_CHUNK, A_CHUNK), dtype=bool))
    eye_req = jnp.eye(n_dec, dtype=F32)

    chunk_v_s, mla_p, mla_s = [], [], []
    lru_h_p, lru_h_s, lru_c_p, lru_c_s = [], [], [], []
    for layer in range(depth):
        kind, j = layer % 3, layer // 3
        g_mix = row(norm_mix[layer])
        if kind == 0:
            w_causal = jnp.where(tril[None], a_w_s[j], 0.0)
            w_small = w_causal[:, :len_s, :len_s]
            wmix_s = jnp.einsum("gts,rq->gtrsq", w_small, eye_req).reshape(A_GROUPS, t_s, t_s)
            pad = A_CHUNK - t_s
            wmix_s = jnp.pad(wmix_s, ((0, 0), (0, pad), (0, pad)))
            bmix_p = a_b_s[j].T
            bmix_s = jnp.pad(jnp.repeat(a_b_s[j].T[:len_s], n_dec, axis=0), ((0, pad), (0, 0)))
            args = (a_w_in[j].astype(BF), row(a_ln_g[j]), row(a_ln_b[j]))
            w_out = a_w_out[j].astype(BF)
            hp, = _mixer_a(hp, g_mix, *args, w_causal.astype(BF), bmix_p, w_out, tm=512,
                           emit_v=False)
            hs, v_s = _mixer_a(hs, g_mix, *args, wmix_s.astype(BF), bmix_s, w_out, tm=t_s,
                               emit_v=True)
            chunk_v_s.append(from_tm(v_s))
        elif kind == 1:
            nope_cols = (jnp.arange(MLA_HEADS)[:, None] * (MLA_NOPE + MLA_ROPE)
                         + jnp.arange(MLA_NOPE)[None, :]).reshape(-1)
            pe_cols = (jnp.arange(MLA_HEADS)[:, None] * (MLA_NOPE + MLA_ROPE) + MLA_NOPE
                       + jnp.arange(MLA_ROPE)[None, :])
            pe_cols_sw = _swap_halves(pe_cols)
            w = dict(
                wdq=b_w_dq[j].astype(BF), qn=row(b_q_norm[j]),
                wuqn=b_w_uq[j][:, nope_cols].astype(BF),
                wuqa=b_w_uq[j][:, pe_cols.reshape(-1)].astype(BF),
                wuqb=b_w_uq[j][:, pe_cols_sw.reshape(-1)].astype(BF),
                wdkv=b_w_dkv[j].astype(BF),
                wdkvb=_swap_halves(b_w_dkv[j][:, MLA_KV_LORA:]).astype(BF),
                kvn=row(b_kv_norm[j]),
                wuk=jnp.transpose(b_w_uk[j], (1, 2, 0)).astype(BF),
            )
            wuv = jnp.transpose(b_w_uv[j], (1, 0, 2)).astype(BF)
            wout = b_w_out[j].astype(BF)

            tq = 256
            kv_p, qlat_p, qpe_p = _mla_proj(hp, g_mix, cos_p, sin_p, w, tq=tq)
            hp = _mla_prompt_attn(qlat_p, qpe_p, kv_p, hp, wuv, wout, n_batch=n_pr, seq=len_p,
                                  tq=tq, tk=512)
            mla_p.append(kv_p.reshape(n_pr, len_p, MLA_LAT))

            kv_s, qlat_s, qpe_s = _mla_proj(hs, g_mix, cos_s, sin_s, w, tq=t_s)

            def per_request(q):
                q = q.reshape(MLA_HEADS, len_s, n_dec, q.shape[-1])
                return jnp.transpose(q, (2, 0, 1, 3)).reshape(n_dec, MLA_HEADS * len_s, q.shape[-1])

            kv_new = jnp.pad(from_tm(kv_s), ((0, 0), (0, 16 - len_s), (0, 0)))
            o_s = _mla_decode_attn(page_table, per_request(qlat_s), per_request(qpe_s), kv_new,
                                   jnp.swapaxes(cache_mla, 2, 3), layer=j, n_new=len_s,
                                   pages_per_step=32)
            o_s = o_s.reshape(n_dec, MLA_HEADS, len_s, MLA_KV_LORA)
            o_s = jnp.transpose(o_s, (1, 2, 0, 3)).reshape(1, MLA_HEADS * t_s, MLA_KV_LORA)
            hs = _mla_out(hs, o_s, wuv, wout, tq=t_s)
            mla_s.append(from_tm(kv_s))
        else:
            w = dict(
                wx=c_w_x[j].astype(BF), wgate=c_w_gate[j].astype(BF), convw=c_conv_w[j],
                convb=row(c_conv_b[j]), wa=c_w_a[j].astype(BF), ba=row(c_b_a[j]),
                wi=c_w_i[j].astype(BF), bi=row(c_b_i[j]), lam=row(c_lambda[j]),
                wout=c_w_out[j].astype(BF),
            )
            hp, hl_p, cb_p = _mixer_c_prompt(hp, g_mix, w, n_batch=n_pr, seq=len_p, tt=256)
            lru_h_p.append(hl_p.reshape(n_pr, LRU_WIDTH))
            lru_c_p.append(cb_p)
            buf0 = jnp.swapaxes(state_lru_conv[j], 0, 1).reshape(-1, LRU_WIDTH)
            hs, hl_s, cb_s = _mixer_c_sample(hs, g_mix, state_lru_h[j], buf0, w, n_req=n_dec)
            lru_h_s.append(hl_s)
            lru_c_s.append(jnp.swapaxes(cb_s.reshape(CONV_WIDTH - 1, n_dec, LRU_WIDTH), 0, 1))

        last = layer == depth - 1
        ffn_w = (row(norm_ffn[layer]), f_wg, f_wu, f_wd, row(norm_out))
        hp = _ffn(hp, *ffn_w, layer=layer, tm=512, final_norm=last)
        hs = _ffn(hs, *ffn_w, layer=layer, tm=t_s, final_norm=last)

    return (hp.reshape(n_pr, len_p, D_MODEL), from_tm(hs), jnp.stack(chunk_v_s),
            jnp.stack(mla_p), jnp.stack(mla_s), jnp.stack(lru_h_p), jnp.stack(lru_h_s),
            jnp.stack(lru_c_p), jnp.stack(lru_c_s))
```

```python
import functools

import jax
import jax.numpy as jnp
import numpy as np
from jax import lax
from jax.experimental import pallas as pl
from jax.experimental.pallas import tpu as pltpu

BF = jnp.bfloat16
F32 = jnp.float32

D_MODEL = 1024
D_FF = 2816
A_CHUNK = 128
A_GROUPS = 8
MLA_HEADS = 8
MLA_Q_LORA = 384
MLA_KV_LORA = 256
MLA_NOPE = 128
MLA_ROPE = 64
MLA_V = 128
MLA_LAT = MLA_KV_LORA + MLA_ROPE
MLA_SCALE = (MLA_NOPE + MLA_ROPE) ** -0.5
ROPE_THETA = 10000.0
PAGE_SIZE = 128
LRU_WIDTH = 1280
LRU_BLOCKS = 10
LRU_BLOCK_DIM = 128
CONV_WIDTH = 4
LRU_C = 8.0
RMS_EPS = 1e-6
LN_EPS = 1e-5

SUBLANES = 8
LANES = 128
LOG2E = 1.4426950408889634
VMEM_LIMIT = 56 * 1024 * 1024


def _params(*sem):
    return pltpu.CompilerParams(dimension_semantics=sem, vmem_limit_bytes=VMEM_LIMIT)


def _dot(a, b):
    return jnp.dot(a.astype(BF), b.astype(BF), preferred_element_type=F32)


def _dot_nt(a, b):
    return lax.dot_general(a.astype(BF), b.astype(BF), (((1,), (1,)), ((), ())),
                           preferred_element_type=F32)


def _rmsnorm(x, g):
    return x * lax.rsqrt(jnp.mean(x * x, axis=-1, keepdims=True) + RMS_EPS) * g


def _layernorm(x, g, b):
    mu = jnp.mean(x, axis=-1, keepdims=True)
    xc = x - mu
    return xc * lax.rsqrt(jnp.mean(xc * xc, axis=-1, keepdims=True) + LN_EPS) * g + b


def _sigmoid(x):
    return 0.5 * jnp.tanh(0.5 * x) + 0.5


def _full(shape):
    return pl.BlockSpec(shape, lambda *_: (0,) * len(shape))


def _ffn_kernel(x_ref, g_ref, wg_ref, wu_ref, wd_ref, gout_ref, o_ref, h_scr, *,
                final_norm, n_split):
    x = x_ref[...]
    xn = _rmsnorm(x, g_ref[...]).astype(BF)
    width = D_FF // n_split
    for c in range(n_split):
        cols = slice(c * width, (c + 1) * width)
        gate = _dot(xn, wg_ref[:, cols])
        up = _dot(xn, wu_ref[:, cols])
        h_scr[:, cols] = (gate * _sigmoid(gate) * up).astype(BF)
    y = x + _dot(h_scr[...], wd_ref[...])
    if final_norm:
        y = _rmsnorm(y, gout_ref[...])
    o_ref[...] = y


def _ffn(h, g, wg, wu, wd, gout, *, layer, tm, n_split, final_norm):
    t = h.shape[0]

    def resident(shape):
        return pl.BlockSpec((None,) + shape, lambda i: (layer, 0, 0),
                            pipeline_mode=pl.Buffered(1))

    return pl.pallas_call(
        functools.partial(_ffn_kernel, final_norm=final_norm, n_split=n_split),
        grid=(t // tm,),
        in_specs=[
            pl.BlockSpec((tm, D_MODEL), lambda i: (i, 0)),
            _full((1, D_MODEL)),
            resident((D_MODEL, D_FF)),
            resident((D_MODEL, D_FF)),
            resident((D_FF, D_MODEL)),
            _full((1, D_MODEL)),
        ],
        out_specs=pl.BlockSpec((tm, D_MODEL), lambda i: (i, 0)),
        out_shape=jax.ShapeDtypeStruct((t, D_MODEL), F32),
        scratch_shapes=[pltpu.VMEM((tm, D_FF), BF)],
        compiler_params=_params("parallel"),
        name="ffn",
    )(h, g, wg, wu, wd, gout)


def _mixer_a_kernel(x_ref, g_ref, win_ref, lng_ref, lnb_ref, wmix_ref, bmix_ref, wout_ref,
                    o_ref, *rest, emit_v, n_sub):
    u_scr, vb_scr, uv_scr = rest[-3:]
    sub = x_ref.shape[0] // n_sub
    for part in range(n_sub):
        tile = slice(part * sub, (part + 1) * sub)
        x = x_ref[tile, :]
        xn = _rmsnorm(x, g_ref[...]).astype(BF)
        u_scr[tile, :] = jax.nn.gelu(_dot(xn, win_ref[:, :D_MODEL]))
        v = _layernorm(jax.nn.gelu(_dot(xn, win_ref[:, D_MODEL:])), lng_ref[...], lnb_ref[...])
        if emit_v:
            rest[0][tile, :] = v
        vb_scr[tile, :] = v.astype(BF)
        for c in range(part * sub // A_CHUNK, (part + 1) * sub // A_CHUNK):
            rows = slice(c * A_CHUNK, (c + 1) * A_CHUNK)
            for g in range(A_GROUPS):
                cols = slice(g * A_CHUNK, (g + 1) * A_CHUNK)
                mixed = _dot(wmix_ref[g], vb_scr[rows, cols]) + bmix_ref[:, g:g + 1]
                uv_scr[rows, cols] = (u_scr[rows, cols] * mixed).astype(BF)
        o_ref[tile, :] = x + _dot(uv_scr[tile, :], wout_ref[...])


def _mixer_a(h, g, win, lng, lnb, wmix, bmix, wout, *, tm, emit_v):
    t = h.shape[0]
    n_out = 2 if emit_v else 1
    return pl.pallas_call(
        functools.partial(_mixer_a_kernel, emit_v=emit_v, n_sub=max(1, tm // (2 * A_CHUNK))),
        grid=(t // tm,),
        in_specs=[
            pl.BlockSpec((tm, D_MODEL), lambda i: (i, 0)),
            _full((1, D_MODEL)),
            _full((D_MODEL, 2 * D_MODEL)),
            _full((1, D_MODEL)),
            _full((1, D_MODEL)),
            _full((A_GROUPS, A_CHUNK, A_CHUNK)),
            _full((A_CHUNK, A_GROUPS)),
            _full((D_MODEL, D_MODEL)),
        ],
        out_specs=[pl.BlockSpec((tm, D_MODEL), lambda i: (i, 0))] * n_out,
        out_shape=[jax.ShapeDtypeStruct((t, D_MODEL), F32)] * n_out,
        scratch_shapes=[pltpu.VMEM((tm, D_MODEL), F32), pltpu.VMEM((tm, D_MODEL), BF),
                        pltpu.VMEM((tm, D_MODEL), BF)],
        compiler_params=_params("parallel"),
        name="mixer_a",
    )(h, g, win, lng, lnb, wmix, bmix, wout)


def _mla_proj_kernel(x_ref, g_ref, cos_ref, sin_ref, wdq_ref, qn_ref, wuqn_ref, wuqa_ref,
                     wuqb_ref, wdkv_ref, wdkvb_ref, kvn_ref, wuk_ref,
                     kv_ref, qlat_ref, qpe_ref):
    tq = x_ref.shape[0]
    xn = _rmsnorm(x_ref[...], g_ref[...]).astype(BF)
    cq = _rmsnorm(_dot(xn, wdq_ref[...]), qn_ref[...]).astype(BF)
    q_nope = _dot(cq, wuqn_ref[...])
    cos = cos_ref[...]
    sin = sin_ref[...]
    q_pe = _dot(cq, wuqa_ref[...]) * cos + _dot(cq, wuqb_ref[...]) * sin
    kv = _dot(xn, wdkv_ref[...])
    kv_ref[:, :MLA_KV_LORA] = _rmsnorm(kv[:, :MLA_KV_LORA], kvn_ref[...])
    kv_ref[:, MLA_KV_LORA:] = (kv[:, MLA_KV_LORA:] * cos[:, :MLA_ROPE]
                               + _dot(xn, wdkvb_ref[...]) * sin[:, :MLA_ROPE])
    for h in range(MLA_HEADS):
        rows = slice(h * tq, (h + 1) * tq)
        qlat_ref[0, rows, :] = _dot(q_nope[:, h * MLA_NOPE:(h + 1) * MLA_NOPE],
                                    wuk_ref[h]).astype(BF)
        qpe_ref[0, rows, :] = q_pe[:, h * MLA_ROPE:(h + 1) * MLA_ROPE].astype(BF)


def _mla_proj(h, g, cos, sin, w, *, tq):
    t = h.shape[0]
    nb = t // tq
    npos = cos.shape[0] // tq
    hq = MLA_HEADS * tq
    return pl.pallas_call(
        _mla_proj_kernel,
        grid=(nb,),
        in_specs=[
            pl.BlockSpec((tq, D_MODEL), lambda i: (i, 0)),
            _full((1, D_MODEL)),
            pl.BlockSpec((tq, MLA_HEADS * MLA_ROPE), lambda i: (i % npos, 0)),
            pl.BlockSpec((tq, MLA_HEADS * MLA_ROPE), lambda i: (i % npos, 0)),
            _full((D_MODEL, MLA_Q_LORA)),
            _full((1, MLA_Q_LORA)),
            _full((MLA_Q_LORA, MLA_HEADS * MLA_NOPE)),
            _full((MLA_Q_LORA, MLA_HEADS * MLA_ROPE)),
            _full((MLA_Q_LORA, MLA_HEADS * MLA_ROPE)),
            _full((D_MODEL, MLA_LAT)),
            _full((D_MODEL, MLA_ROPE)),
            _full((1, MLA_KV_LORA)),
            _full((MLA_HEADS, MLA_NOPE, MLA_KV_LORA)),
        ],
        out_specs=[pl.BlockSpec((tq, MLA_LAT), lambda i: (i, 0)),
                   pl.BlockSpec((1, hq, MLA_KV_LORA), lambda i: (i, 0, 0)),
                   pl.BlockSpec((1, hq, MLA_ROPE), lambda i: (i, 0, 0))],
        out_shape=[jax.ShapeDtypeStruct((t, MLA_LAT), F32),
                   jax.ShapeDtypeStruct((nb, hq, MLA_KV_LORA), BF),
                   jax.ShapeDtypeStruct((nb, hq, MLA_ROPE), BF)],
        compiler_params=_params("parallel"),
        name="mla_proj",
    )(h, g, cos, sin, w["wdq"], w["qn"], w["wuqn"], w["wuqa"], w["wuqb"], w["wdkv"],
      w["wdkvb"], w["kvn"], w["wuk"])


def _lanes(x, n):
    return x[:, :n] if n <= LANES else jnp.concatenate([x] * (n // LANES), axis=-1)


def _softmax_step(s, pv, m_scr, l_scr, acc_scr, rows):
    c2 = MLA_SCALE * LOG2E
    m_prev = m_scr[rows, :]
    m_new = jnp.maximum(m_prev, jnp.max(s, axis=-1, keepdims=True))
    alpha = jnp.exp2((m_prev - m_new) * c2)
    p = jnp.exp2((s - _lanes(m_new, s.shape[-1])) * c2)
    l_scr[rows, :] = alpha * l_scr[rows, :] + jnp.sum(p, axis=-1, keepdims=True)
    acc_scr[rows, :] = _lanes(alpha, MLA_KV_LORA) * acc_scr[rows, :] + pv(p)
    m_scr[rows, :] = m_new


def _mla_value_out(o_head, x, wuv_ref, wout_ref, ov_scr):
    for h in range(MLA_HEADS):
        ov_scr[:, h * MLA_V:(h + 1) * MLA_V] = _dot(o_head(h), wuv_ref[h]).astype(BF)
    return x + _dot(ov_scr[...], wout_ref[...])


def _mla_prompt_attn_kernel(qi_ref, kj_ref, qlat_ref, qpe_ref, kv_ref, x_ref, wuv_ref, wout_ref,
                            y_ref, m_scr, l_scr, acc_scr, ov_scr, *, tq, tk, rb):
    step_id = pl.program_id(1)
    i = qi_ref[step_id]
    j = kj_ref[step_id]
    j_diag = ((i + 1) * tq - 1) // tk

    @pl.when(j == 0)
    def _():
        m_scr[...] = jnp.full_like(m_scr, -jnp.inf)
        l_scr[...] = jnp.zeros_like(l_scr)
        acc_scr[...] = jnp.zeros_like(acc_scr)

    def step(masked):
        kb = kv_ref[...].astype(BF)
        klat, kpe = kb[:, :MLA_KV_LORA], kb[:, MLA_KV_LORA:]
        for r in range(MLA_HEADS * tq // rb):
            rows = slice(r * rb, (r + 1) * rb)
            s = _dot_nt(qlat_ref[0, rows, :], klat) + _dot_nt(qpe_ref[0, rows, :], kpe)
            if masked:
                tok = lax.broadcasted_iota(jnp.int32, (rb, tk), 0) & (tq - 1)
                k_pos = j * tk + lax.broadcasted_iota(jnp.int32, (rb, tk), 1)
                s = jnp.where(k_pos <= i * tq + tok, s, -jnp.inf)
            _softmax_step(s, lambda p: _dot(p, klat), m_scr, l_scr, acc_scr, rows)

    @pl.when(j < j_diag)
    def _():
        step(False)

    @pl.when(j == j_diag)
    def _():
        step(True)

        def o_head(h):
            rows = slice(h * tq, (h + 1) * tq)
            return acc_scr[rows, :] / _lanes(l_scr[rows, :], MLA_KV_LORA)

        y_ref[...] = _mla_value_out(o_head, x_ref[...], wuv_ref, wout_ref, ov_scr)


def _causal_schedule(nq, tq, tk):
    pairs = [(i, j) for i in range(nq) for j in range(((i + 1) * tq - 1) // tk + 1)]
    qi, kj = zip(*pairs)
    return np.asarray(qi, np.int32), np.asarray(kj, np.int32)


def _mla_prompt_attn(qlat, qpe, kv, h, wuv, wout, *, n_batch, seq, tq, tk):
    nq, nk = seq // tq, seq // tk
    hq = MLA_HEADS * tq
    qi, kj = _causal_schedule(nq, tq, tk)

    def q_map(b, s, qi_ref, kj_ref):
        return (b * nq + qi_ref[s], 0, 0)

    def tok_map(b, s, qi_ref, kj_ref):
        return (b * nq + qi_ref[s], 0)

    grid_spec = pltpu.PrefetchScalarGridSpec(
        num_scalar_prefetch=2,
        grid=(n_batch, len(qi)),
        in_specs=[
            pl.BlockSpec((1, hq, MLA_KV_LORA), q_map),
            pl.BlockSpec((1, hq, MLA_ROPE), q_map),
            pl.BlockSpec((tk, MLA_LAT), lambda b, s, qi_ref, kj_ref: (b * nk + kj_ref[s], 0)),
            pl.BlockSpec((tq, D_MODEL), tok_map),
            _full((MLA_HEADS, MLA_KV_LORA, MLA_V)),
            _full((MLA_HEADS * MLA_V, D_MODEL)),
        ],
        out_specs=pl.BlockSpec((tq, D_MODEL), tok_map),
        scratch_shapes=[pltpu.VMEM((hq, LANES), F32), pltpu.VMEM((hq, LANES), F32),
                        pltpu.VMEM((hq, MLA_KV_LORA), F32),
                        pltpu.VMEM((tq, MLA_HEADS * MLA_V), BF)],
    )
    return pl.pallas_call(
        functools.partial(_mla_prompt_attn_kernel, tq=tq, tk=tk, rb=512),
        grid_spec=grid_spec,
        out_shape=jax.ShapeDtypeStruct((n_batch * seq, D_MODEL), F32),
        compiler_params=_params("parallel", "arbitrary"),
        name="mla_prompt_attn",
    )(qi, kj, qlat, qpe, kv, h, wuv, wout)


def _mla_decode_attn_kernel(pt_ref, qlat_ref, qpe_ref, kvnew_ref, *rest, n_pages, n_new):
    del pt_ref
    page_refs = rest[:n_pages]
    o_ref, kbuf, m_scr, l_scr, acc_scr = rest[n_pages:]
    s_id = pl.program_id(1)
    qlat = qlat_ref[0]
    qpe = qpe_ref[0]
    every_row = slice(None)

    @pl.when(s_id == 0)
    def _():
        m_scr[...] = jnp.full_like(m_scr, -jnp.inf)
        l_scr[...] = jnp.zeros_like(l_scr)
        acc_scr[...] = jnp.zeros_like(acc_scr)
        kn = kvnew_ref[0].astype(BF)
        klat, kpe = kn[:, :MLA_KV_LORA], kn[:, MLA_KV_LORA:]
        s = _dot_nt(qlat, klat) + _dot_nt(qpe, kpe)
        tok = lax.broadcasted_iota(jnp.int32, s.shape, 0) & (n_new - 1)
        col = lax.broadcasted_iota(jnp.int32, s.shape, 1)
        s = jnp.where(col <= tok, s, -jnp.inf)
        _softmax_step(s, lambda p: _dot(p, klat), m_scr, l_scr, acc_scr, every_row)

    for p_id in range(n_pages):
        kbuf[:, p_id * PAGE_SIZE:(p_id + 1) * PAGE_SIZE] = page_refs[p_id][...].astype(BF)
    klat_t = kbuf[:MLA_KV_LORA, :]
    s = _dot(qlat, klat_t) + _dot(qpe, kbuf[MLA_KV_LORA:, :])
    _softmax_step(s, lambda p: _dot_nt(p, klat_t), m_scr, l_scr, acc_scr, every_row)

    @pl.when(s_id == pl.num_programs(1) - 1)
    def _():
        o_ref[0] = acc_scr[...] / _lanes(l_scr[...], MLA_KV_LORA)


def _mla_decode_attn(page_table, qlat, qpe, kvnew, cache_t, *, layer, n_new, pages_per_step):
    n_req, n_pages_total = page_table.shape
    rows = qlat.shape[1]
    steps = n_pages_total // pages_per_step

    def page_map(p_id):
        return lambda r, s, pt: (layer, pt[r, s * pages_per_step + p_id], 0, 0)

    grid_spec = pltpu.PrefetchScalarGridSpec(
        num_scalar_prefetch=1,
        grid=(n_req, steps),
        in_specs=[
            pl.BlockSpec((1, rows, MLA_KV_LORA), lambda r, s, pt: (r, 0, 0)),
            pl.BlockSpec((1, rows, MLA_ROPE), lambda r, s, pt: (r, 0, 0)),
            pl.BlockSpec((1,) + kvnew.shape[1:], lambda r, s, pt: (r, 0, 0)),
        ] + [pl.BlockSpec((None, None, MLA_LAT, PAGE_SIZE), page_map(p_id))
             for p_id in range(pages_per_step)],
        out_specs=pl.BlockSpec((1, rows, MLA_KV_LORA), lambda r, s, pt: (r, 0, 0)),
        scratch_shapes=[pltpu.VMEM((MLA_LAT, pages_per_step * PAGE_SIZE), BF),
                        pltpu.VMEM((rows, LANES), F32), pltpu.VMEM((rows, LANES), F32),
                        pltpu.VMEM((rows, MLA_KV_LORA), F32)],
    )
    return pl.pallas_call(
        functools.partial(_mla_decode_attn_kernel, n_pages=pages_per_step, n_new=n_new),
        grid_spec=grid_spec,
        out_shape=jax.ShapeDtypeStruct((n_req, rows, MLA_KV_LORA), F32),
        compiler_params=_params("parallel", "arbitrary"),
        name="mla_decode_attn",
    )(page_table, qlat, qpe, kvnew, *([cache_t] * pages_per_step))


def _mla_out_kernel(x_ref, o_ref, wuv_ref, wout_ref, y_ref, ov_scr):
    tq = x_ref.shape[0]
    y_ref[...] = _mla_value_out(lambda h: o_ref[0, h * tq:(h + 1) * tq, :], x_ref[...],
                                wuv_ref, wout_ref, ov_scr)


def _mla_out(h, o, wuv, wout, *, tq):
    t = h.shape[0]
    hq = MLA_HEADS * tq
    return pl.pallas_call(
        _mla_out_kernel,
        grid=(t // tq,),
        in_specs=[
            pl.BlockSpec((tq, D_MODEL), lambda i: (i, 0)),
            pl.BlockSpec((1, hq, MLA_KV_LORA), lambda i: (i, 0, 0)),
            _full((MLA_HEADS, MLA_KV_LORA, MLA_V)),
            _full((MLA_HEADS * MLA_V, D_MODEL)),
        ],
        out_specs=pl.BlockSpec((tq, D_MODEL), lambda i: (i, 0)),
        out_shape=jax.ShapeDtypeStruct((t, D_MODEL), F32),
        scratch_shapes=[pltpu.VMEM((tq, MLA_HEADS * MLA_V), BF)],
        compiler_params=_params("parallel"),
        name="mla_out",
    )(h, o, wuv, wout)


def _block_diag(x, w_ref, b):
    parts = [_dot(x[:, n * LRU_BLOCK_DIM:(n + 1) * LRU_BLOCK_DIM], w_ref[n])
             for n in range(LRU_BLOCKS)]
    return jnp.concatenate(parts, axis=-1) + b


def _lru_coeffs(xc, wa_ref, ba_ref, wi_ref, bi_ref, lam_ref):
    r = _sigmoid(_block_diag(xc, wa_ref, ba_ref[...]))
    gi = _sigmoid(_block_diag(xc, wi_ref, bi_ref[...]))
    neg_lam = -lam_ref[...]
    softplus = jnp.maximum(neg_lam, 0.0) + jnp.log1p(jnp.exp(-jnp.abs(neg_lam)))
    a = jnp.exp((-LRU_C * softplus) * r)
    return a, jnp.sqrt(1.0 - a * a) * gi * xc


def _scan_pitch(seg):
    pitch = seg + SUBLANES
    return pitch if (pitch // SUBLANES) % 2 else pitch + SUBLANES


def _mixer_c_prompt_kernel(x_ref, g_ref, wx_ref, wgate_ref, convw_ref, convb_ref, wa_ref, ba_ref,
                           wi_ref, bi_ref, lam_ref, wout_ref,
                           o_ref, hlast_ref, buf_ref,
                           xx_scr, gate_scr, a_scr, b_scr, hloc_scr, ploc_scr, carry_scr):
    tt = x_ref.shape[0]
    seg = tt // SUBLANES
    pitch = _scan_pitch(seg)
    ti = pl.program_id(1)

    @pl.when(ti == 0)
    def _():
        xx_scr[0:SUBLANES, :] = jnp.zeros((SUBLANES, LRU_WIDTH), F32)
        carry_scr[...] = jnp.zeros_like(carry_scr)

    x = x_ref[...]
    xn = _rmsnorm(x, g_ref[...]).astype(BF)
    gate_scr[...] = jax.nn.gelu(_dot(xn, wgate_ref[...]))
    xx_scr[SUBLANES:, :] = _dot(xn, wx_ref[...])
    xc = convb_ref[...]
    for k in range(CONV_WIDTH):
        off = SUBLANES - (CONV_WIDTH - 1) + k
        xc = xc + convw_ref[k:k + 1, :] * xx_scr[off:off + tt, :]
    buf_ref[0] = xx_scr[tt + SUBLANES - (CONV_WIDTH - 1):, :]
    xx_scr[0:SUBLANES, :] = xx_scr[tt:tt + SUBLANES, :]

    a, b = _lru_coeffs(xc, wa_ref, ba_ref, wi_ref, bi_ref, lam_ref)
    for n in range(LRU_BLOCKS):
        cols = slice(n * LRU_BLOCK_DIM, (n + 1) * LRU_BLOCK_DIM)
        for k in range(SUBLANES):
            a_scr[n, k * pitch:k * pitch + seg, :] = a[k * seg:(k + 1) * seg, cols]
            b_scr[n, k * pitch:k * pitch + seg, :] = b[k * seg:(k + 1) * seg, cols]

    def body(j, carry):
        hs, ps = carry
        rows = pl.ds(j, SUBLANES, stride=pitch)
        new_h, new_p = [], []
        for n in range(LRU_BLOCKS):
            aj = a_scr[n, rows, :]
            h = aj * hs[n] + b_scr[n, rows, :]
            p = aj * ps[n]
            hloc_scr[n, rows, :] = h
            ploc_scr[n, rows, :] = p
            new_h.append(h)
            new_p.append(p)
        return tuple(new_h), tuple(new_p)

    h_end, p_end = lax.fori_loop(
        0, seg, body,
        ((jnp.zeros((SUBLANES, LRU_BLOCK_DIM), F32),) * LRU_BLOCKS,
         (jnp.ones((SUBLANES, LRU_BLOCK_DIM), F32),) * LRU_BLOCKS))

    for n in range(LRU_BLOCKS):
        cols = slice(n * LRU_BLOCK_DIM, (n + 1) * LRU_BLOCK_DIM)
        c = carry_scr[:, cols]
        for k in range(SUBLANES):
            rows = slice(k * seg, (k + 1) * seg)
            srows = slice(k * pitch, k * pitch + seg)
            gate_scr[rows, cols] = gate_scr[rows, cols] * (hloc_scr[n, srows, :]
                                                           + ploc_scr[n, srows, :] * c)
            c = p_end[n][k:k + 1, :] * c + h_end[n][k:k + 1, :]
        carry_scr[:, cols] = c
    hlast_ref[0] = carry_scr[...]
    o_ref[...] = x + _dot(gate_scr[...], wout_ref[...])


def _mixer_c_prompt(h, g, w, *, n_batch, seq, tt):
    nt = seq // tt
    wd = LRU_WIDTH
    blk = (LRU_BLOCKS, LRU_BLOCK_DIM, LRU_BLOCK_DIM)
    return pl.pallas_call(
        _mixer_c_prompt_kernel,
        grid=(n_batch, nt),
        in_specs=[
            pl.BlockSpec((tt, D_MODEL), lambda b, i: (b * nt + i, 0)),
            _full((1, D_MODEL)),
            _full((D_MODEL, wd)), _full((D_MODEL, wd)),
            _full((CONV_WIDTH, wd)), _full((1, wd)),
            _full(blk), _full((1, wd)), _full(blk), _full((1, wd)), _full((1, wd)),
            _full((wd, D_MODEL)),
        ],
        out_specs=[pl.BlockSpec((tt, D_MODEL), lambda b, i: (b * nt + i, 0)),
                   pl.BlockSpec((1, 1, wd), lambda b, i: (b, 0, 0)),
                   pl.BlockSpec((1, CONV_WIDTH - 1, wd), lambda b, i: (b, 0, 0))],
        out_shape=[jax.ShapeDtypeStruct((n_batch * seq, D_MODEL), F32),
                   jax.ShapeDtypeStruct((n_batch, 1, wd), F32),
                   jax.ShapeDtypeStruct((n_batch, CONV_WIDTH - 1, wd), F32)],
        scratch_shapes=[pltpu.VMEM((tt + SUBLANES, wd), F32), pltpu.VMEM((tt, wd), F32)] +
                       [pltpu.VMEM((LRU_BLOCKS, SUBLANES * _scan_pitch(tt // SUBLANES),
                                    LRU_BLOCK_DIM), F32)] * 4 +
                       [pltpu.VMEM((1, wd), F32)],
        compiler_params=_params("parallel", "arbitrary"),
        name="mixer_c_prompt",
    )(h, g, w["wx"], w["wgate"], w["convw"], w["convb"], w["wa"], w["ba"], w["wi"], w["bi"],
      w["lam"], w["wout"])


def _mixer_c_sample_kernel(x_ref, g_ref, h0_ref, buf0_ref, wx_ref, wgate_ref, convw_ref, convb_ref,
                           wa_ref, ba_ref, wi_ref, bi_ref, lam_ref, wout_ref,
                           o_ref, hlast_ref, buf_ref, xx_scr, gh_scr, *, n_req):
    t = x_ref.shape[0]
    hist = (CONV_WIDTH - 1) * n_req
    x = x_ref[...]
    xn = _rmsnorm(x, g_ref[...]).astype(BF)
    gate = jax.nn.gelu(_dot(xn, wgate_ref[...]))
    xx_scr[0:hist, :] = buf0_ref[...]
    xx_scr[hist:, :] = _dot(xn, wx_ref[...])
    xc = convb_ref[...]
    for k in range(CONV_WIDTH):
        xc = xc + convw_ref[k:k + 1, :] * xx_scr[k * n_req:k * n_req + t, :]
    buf_ref[...] = xx_scr[t:, :]
    a, b = _lru_coeffs(xc, wa_ref, ba_ref, wi_ref, bi_ref, lam_ref)
    h = h0_ref[...]
    for step in range(t // n_req):
        rows = slice(step * n_req, (step + 1) * n_req)
        h = a[rows, :] * h + b[rows, :]
        gh_scr[rows, :] = gate[rows, :] * h
    hlast_ref[...] = h
    o_ref[...] = x + _dot(gh_scr[...], wout_ref[...])


def _mixer_c_sample(h, g, h0, buf0, w, *, n_req):
    t = h.shape[0]
    wd = LRU_WIDTH
    hist = (CONV_WIDTH - 1) * n_req
    blk = (LRU_BLOCKS, LRU_BLOCK_DIM, LRU_BLOCK_DIM)
    return pl.pallas_call(
        functools.partial(_mixer_c_sample_kernel, n_req=n_req),
        grid=(1,),
        in_specs=[
            _full((t, D_MODEL)), _full((1, D_MODEL)), _full((n_req, wd)), _full((hist, wd)),
            _full((D_MODEL, wd)), _full((D_MODEL, wd)),
            _full((CONV_WIDTH, wd)), _full((1, wd)),
            _full(blk), _full((1, wd)), _full(blk), _full((1, wd)), _full((1, wd)),
            _full((wd, D_MODEL)),
        ],
        out_specs=[_full((t, D_MODEL)), _full((n_req, wd)), _full((hist, wd))],
        out_shape=[jax.ShapeDtypeStruct((t, D_MODEL), F32),
                   jax.ShapeDtypeStruct((n_req, wd), F32),
                   jax.ShapeDtypeStruct((hist, wd), F32)],
        scratch_shapes=[pltpu.VMEM((hist + t, wd), F32), pltpu.VMEM((t, wd), F32)],
        compiler_params=_params("arbitrary"),
        name="mixer_c_sample",
    )(h, g, h0, buf0, w["wx"], w["wgate"], w["convw"], w["convb"], w["wa"], w["ba"], w["wi"],
      w["bi"], w["lam"], w["wout"])


def _rope_tables(pos):
    inv = 1.0 / (ROPE_THETA ** (jnp.arange(0, MLA_ROPE, 2, dtype=F32) / MLA_ROPE))
    ang = pos.astype(F32)[:, None] * inv[None, :]
    cos, sin = jnp.cos(ang), jnp.sin(ang)
    cos_t = jnp.tile(jnp.concatenate([cos, cos], axis=-1), (1, MLA_HEADS))
    sin_t = jnp.tile(jnp.concatenate([-sin, sin], axis=-1), (1, MLA_HEADS))
    return cos_t, sin_t


def _swap_halves(w):
    half = w.shape[-1] // 2
    return jnp.concatenate([w[..., half:], w[..., :half]], axis=-1)


def kernel(x_prompt, x_sample, cache_mla, state_lru_h, state_lru_conv, page_table, norm_mix, norm_ffn, norm_out, a_w_in, a_ln_g, a_ln_b, a_w_s, a_b_s, a_w_out, b_w_dq, b_q_norm, b_w_uq, b_w_dkv, b_kv_norm, b_w_uk, b_w_uv, b_w_out, c_w_x, c_w_gate, c_conv_w, c_conv_b, c_w_a, c_b_a, c_w_i, c_b_i, c_lambda, c_w_out, f_w_gate, f_w_up, f_w_down):
    n_pr, len_p, _ = x_prompt.shape
    n_dec, len_s, _ = x_sample.shape
    depth = norm_mix.shape[0]
    past_len = page_table.shape[1] * PAGE_SIZE
    t_s = n_dec * len_s
    assert len_p % A_CHUNK == 0 and len_s <= A_CHUNK and (len_s & (len_s - 1)) == 0

    def row(v):
        return v.reshape(1, -1)

    def to_tm(v):
        return jnp.swapaxes(v, 0, 1).reshape((t_s,) + v.shape[2:])

    def from_tm(v):
        return jnp.swapaxes(v.reshape((len_s, n_dec) + v.shape[1:]), 0, 1)

    hp = x_prompt.reshape(n_pr * len_p, D_MODEL)
    hs = to_tm(x_sample)

    cos_p, sin_p = _rope_tables(jnp.arange(len_p, dtype=jnp.int32))
    pos_s = past_len + jnp.arange(len_s, dtype=jnp.int32)
    cos_s, sin_s = _rope_tables(jnp.repeat(pos_s, n_dec))

    f_wg, f_wu, f_wd = f_w_gate.astype(BF), f_w_up.astype(BF), f_w_down.astype(BF)
    tril = jnp.tril(jnp.ones((A_CHUNK, A_CHUNK), dtype=bool))
    eye_req = jnp.eye(n_dec, dtype=F32)

    chunk_v_s, mla_p, mla_s = [], [], []
    lru_h_p, lru_h_s, lru_c_p, lru_c_s = [], [], [], []
    for layer in range(depth):
        kind, j = layer % 3, layer // 3
        g_mix = row(norm_mix[layer])
        if kind == 0:
            w_causal = jnp.where(tril[None], a_w_s[j], 0.0)
            w_small = w_causal[:, :len_s, :len_s]
            wmix_s = jnp.einsum("gts,rq->gtrsq", w_small, eye_req).reshape(A_GROUPS, t_s, t_s)
            pad = A_CHUNK - t_s
            wmix_s = jnp.pad(wmix_s, ((0, 0), (0, pad), (0, pad)))
            bmix_p = a_b_s[j].T
            bmix_s = jnp.pad(jnp.repeat(a_b_s[j].T[:len_s], n_dec, axis=0), ((0, pad), (0, 0)))
            args = (a_w_in[j].astype(BF), row(a_ln_g[j]), row(a_ln_b[j]))
            w_out = a_w_out[j].astype(BF)
            hp, = _mixer_a(hp, g_mix, *args, w_causal.astype(BF), bmix_p, w_out, tm=512,
                           emit_v=False)
            hs, v_s = _mixer_a(hs, g_mix, *args, wmix_s.astype(BF), bmix_s, w_out, tm=t_s,
                               emit_v=True)
            chunk_v_s.append(from_tm(v_s))
        elif kind == 1:
            nope_cols = (jnp.arange(MLA_HEADS)[:, None] * (MLA_NOPE + MLA_ROPE)
                         + jnp.arange(MLA_NOPE)[None, :]).reshape(-1)
            pe_cols = (jnp.arange(MLA_HEADS)[:, None] * (MLA_NOPE + MLA_ROPE) + MLA_NOPE
                       + jnp.arange(MLA_ROPE)[None, :])
            pe_cols_sw = _swap_halves(pe_cols)
            w = dict(
                wdq=b_w_dq[j].astype(BF), qn=row(b_q_norm[j]),
                wuqn=b_w_uq[j][:, nope_cols].astype(BF),
                wuqa=b_w_uq[j][:, pe_cols.reshape(-1)].astype(BF),
                wuqb=b_w_uq[j][:, pe_cols_sw.reshape(-1)].astype(BF),
                wdkv=b_w_dkv[j].astype(BF),
                wdkvb=_swap_halves(b_w_dkv[j][:, MLA_KV_LORA:]).astype(BF),
                kvn=row(b_kv_norm[j]),
                wuk=jnp.transpose(b_w_uk[j], (1, 2, 0)).astype(BF),
            )
            wuv = jnp.transpose(b_w_uv[j], (1, 0, 2)).astype(BF)
            wout = b_w_out[j].astype(BF)

            tq = 512
            kv_p, qlat_p, qpe_p = _mla_proj(hp, g_mix, cos_p, sin_p, w, tq=tq)
            hp = _mla_prompt_attn(qlat_p, qpe_p, kv_p, hp, wuv, wout, n_batch=n_pr, seq=len_p,
                                  tq=tq, tk=512)
            mla_p.append(kv_p.reshape(n_pr, len_p, MLA_LAT))

            kv_s, qlat_s, qpe_s = _mla_proj(hs, g_mix, cos_s, sin_s, w, tq=t_s)

            def per_request(q):
                q = q.reshape(MLA_HEADS, len_s, n_dec, q.shape[-1])
                return jnp.transpose(q, (2, 0, 1, 3)).reshape(n_dec, MLA_HEADS * len_s, q.shape[-1])

            kv_new = jnp.pad(from_tm(kv_s), ((0, 0), (0, 16 - len_s), (0, 0)))
            o_s = _mla_decode_attn(page_table, per_request(qlat_s), per_request(qpe_s), kv_new,
                                   jnp.swapaxes(cache_mla, 2, 3), layer=j, n_new=len_s,
                                   pages_per_step=64)
            o_s = o_s.reshape(n_dec, MLA_HEADS, len_s, MLA_KV_LORA)
            o_s = jnp.transpose(o_s, (1, 2, 0, 3)).reshape(1, MLA_HEADS * t_s, MLA_KV_LORA)
            hs = _mla_out(hs, o_s, wuv, wout, tq=t_s)
            mla_s.append(from_tm(kv_s))
        else:
            w = dict(
                wx=c_w_x[j].astype(BF), wgate=c_w_gate[j].astype(BF), convw=c_conv_w[j],
                convb=row(c_conv_b[j]), wa=c_w_a[j].astype(BF), ba=row(c_b_a[j]),
                wi=c_w_i[j].astype(BF), bi=row(c_b_i[j]), lam=row(c_lambda[j]),
                wout=c_w_out[j].astype(BF),
            )
            hp, hl_p, cb_p = _mixer_c_prompt(hp, g_mix, w, n_batch=n_pr, seq=len_p, tt=256)
            lru_h_p.append(hl_p.reshape(n_pr, LRU_WIDTH))
            lru_c_p.append(cb_p)
            buf0 = jnp.swapaxes(state_lru_conv[j], 0, 1).reshape(-1, LRU_WIDTH)
            hs, hl_s, cb_s = _mixer_c_sample(hs, g_mix, state_lru_h[j], buf0, w, n_req=n_dec)
            lru_h_s.append(hl_s)
            lru_c_s.append(jnp.swapaxes(cb_s.reshape(CONV_WIDTH - 1, n_dec, LRU_WIDTH), 0, 1))

        last = layer == depth - 1
        ffn_w = (row(norm_ffn[layer]), f_wg, f_wu, f_wd, row(norm_out))
        hp = _ffn(hp, *ffn_w, layer=layer, tm=1024, n_split=11, final_norm=last)
        hs = _ffn(hs, *ffn_w, layer=layer, tm=t_s, n_split=2, final_norm=last)

    return (hp.reshape(n_pr, len_p, D_MODEL), from_tm(hs), jnp.stack(chunk_v_s),
            jnp.stack(mla_p), jnp.stack(mla_s), jnp.stack(lru_h_p), jnp.stack(lru_h_s),
            jnp.stack(lru_c_p), jnp.stack(lru_c_s))
```

```python
import functools

import jax
import jax.numpy as jnp
import numpy as np
from jax import lax
from jax.experimental import pallas as pl
from jax.experimental.pallas import tpu as pltpu

BF = jnp.bfloat16
F32 = jnp.float32

D_MODEL = 1024
D_FF = 2816
A_CHUNK = 128
A_GROUPS = 8
MLA_HEADS = 8
MLA_Q_LORA = 384
MLA_KV_LORA = 256
MLA_NOPE = 128
MLA_ROPE = 64
MLA_V = 128
MLA_LAT = MLA_KV_LORA + MLA_ROPE
MLA_SCALE = (MLA_NOPE + MLA_ROPE) ** -0.5
ROPE_THETA = 10000.0
PAGE_SIZE = 128
LRU_WIDTH = 1280
LRU_BLOCKS = 10
LRU_BLOCK_DIM = 128
CONV_WIDTH = 4
LRU_C = 8.0
RMS_EPS = 1e-6
LN_EPS = 1e-5

SUBLANES = 8
LANES = 128
LOG2E = 1.4426950408889634
VMEM_LIMIT = 56 * 1024 * 1024


def _params(*sem):
    return pltpu.CompilerParams(dimension_semantics=sem, vmem_limit_bytes=VMEM_LIMIT)


def _dot(a, b):
    return jnp.dot(a.astype(BF), b.astype(BF), preferred_element_type=F32)


def _dot_nt(a, b):
    return lax.dot_general(a.astype(BF), b.astype(BF), (((1,), (1,)), ((), ())),
                           preferred_element_type=F32)


def _rmsnorm(x, g):
    return x * lax.rsqrt(jnp.mean(x * x, axis=-1, keepdims=True) + RMS_EPS) * g


def _layernorm(x, g, b):
    mu = jnp.mean(x, axis=-1, keepdims=True)
    xc = x - mu
    return xc * lax.rsqrt(jnp.mean(xc * xc, axis=-1, keepdims=True) + LN_EPS) * g + b


def _sigmoid(x):
    return 0.5 * jnp.tanh(0.5 * x) + 0.5


def _full(shape):
    return pl.BlockSpec(shape, lambda *_: (0,) * len(shape))


def _ffn_kernel(x_ref, g_ref, wg_ref, wu_ref, wd_ref, gout_ref, o_ref, h_scr, *,
                final_norm, n_split):
    x = x_ref[...]
    xn = _rmsnorm(x, g_ref[...]).astype(BF)
    width = D_FF // n_split
    for c in range(n_split):
        cols = slice(c * width, (c + 1) * width)
        gate = _dot(xn, wg_ref[:, cols])
        up = _dot(xn, wu_ref[:, cols])
        h_scr[:, cols] = (gate * _sigmoid(gate) * up).astype(BF)
    y = x + _dot(h_scr[...], wd_ref[...])
    if final_norm:
        y = _rmsnorm(y, gout_ref[...])
    o_ref[...] = y


def _ffn(h, g, wg, wu, wd, gout, *, layer, tm, n_split, final_norm):
    t = h.shape[0]

    def resident(shape):
        return pl.BlockSpec((None,) + shape, lambda i: (layer, 0, 0),
                            pipeline_mode=pl.Buffered(1))

    return pl.pallas_call(
        functools.partial(_ffn_kernel, final_norm=final_norm, n_split=n_split),
        grid=(t // tm,),
        in_specs=[
            pl.BlockSpec((tm, D_MODEL), lambda i: (i, 0)),
            _full((1, D_MODEL)),
            resident((D_MODEL, D_FF)),
            resident((D_MODEL, D_FF)),
            resident((D_FF, D_MODEL)),
            _full((1, D_MODEL)),
        ],
        out_specs=pl.BlockSpec((tm, D_MODEL), lambda i: (i, 0)),
        out_shape=jax.ShapeDtypeStruct((t, D_MODEL), F32),
        scratch_shapes=[pltpu.VMEM((tm, D_FF), BF)],
        compiler_params=_params("parallel"),
        name="ffn",
    )(h, g, wg, wu, wd, gout)


def _mixer_a_kernel(x_ref, g_ref, win_ref, lng_ref, lnb_ref, wmix_ref, bmix_ref, wout_ref,
                    o_ref, *rest, emit_v, n_sub):
    u_scr, vb_scr, uv_scr = rest[-3:]
    sub = x_ref.shape[0] // n_sub

    def tile(part):
        return slice(part * sub, (part + 1) * sub)

    def project_in(part):
        xn = _rmsnorm(x_ref[tile(part), :], g_ref[...]).astype(BF)
        u_scr[tile(part), :] = jax.nn.gelu(_dot(xn, win_ref[:, :D_MODEL]))
        v = _layernorm(jax.nn.gelu(_dot(xn, win_ref[:, D_MODEL:])), lng_ref[...], lnb_ref[...])
        if emit_v:
            rest[0][tile(part), :] = v
        vb_scr[tile(part), :] = v.astype(BF)

    def mix(part):
        for c in range(part * sub // A_CHUNK, (part + 1) * sub // A_CHUNK):
            rows = slice(c * A_CHUNK, (c + 1) * A_CHUNK)
            for g in range(A_GROUPS):
                cols = slice(g * A_CHUNK, (g + 1) * A_CHUNK)
                mixed = _dot(wmix_ref[g], vb_scr[rows, cols]) + bmix_ref[:, g:g + 1]
                uv_scr[rows, cols] = (u_scr[rows, cols] * mixed).astype(BF)

    def project_out(part):
        o_ref[tile(part), :] = x_ref[tile(part), :] + _dot(uv_scr[tile(part), :], wout_ref[...])

    for t in range(n_sub + 2):
        if t < n_sub:
            project_in(t)
        if 1 <= t <= n_sub:
            mix(t - 1)
        if t >= 2:
            project_out(t - 2)


def _mixer_a(h, g, win, lng, lnb, wmix, bmix, wout, *, tm, emit_v):
    t = h.shape[0]
    n_out = 2 if emit_v else 1
    return pl.pallas_call(
        functools.partial(_mixer_a_kernel, emit_v=emit_v, n_sub=max(1, tm // (2 * A_CHUNK))),
        grid=(t // tm,),
        in_specs=[
            pl.BlockSpec((tm, D_MODEL), lambda i: (i, 0)),
            _full((1, D_MODEL)),
            _full((D_MODEL, 2 * D_MODEL)),
            _full((1, D_MODEL)),
            _full((1, D_MODEL)),
            _full((A_GROUPS, A_CHUNK, A_CHUNK)),
            _full((A_CHUNK, A_GROUPS)),
            _full((D_MODEL, D_MODEL)),
        ],
        out_specs=[pl.BlockSpec((tm, D_MODEL), lambda i: (i, 0))] * n_out,
        out_shape=[jax.ShapeDtypeStruct((t, D_MODEL), F32)] * n_out,
        scratch_shapes=[pltpu.VMEM((tm, D_MODEL), F32), pltpu.VMEM((tm, D_MODEL), BF),
                        pltpu.VMEM((tm, D_MODEL), BF)],
        compiler_params=_params("parallel"),
        name="mixer_a",
    )(h, g, win, lng, lnb, wmix, bmix, wout)


def _mla_proj_kernel(x_ref, g_ref, cos_ref, sin_ref, wdq_ref, qn_ref, wuqn_ref, wuqa_ref,
                     wuqb_ref, wdkv_ref, wdkvb_ref, kvn_ref, wuk_ref,
                     kv_ref, qa_ref, qb_ref, *, feature_major):
    tq = x_ref.shape[0]
    xn = _rmsnorm(x_ref[...], g_ref[...]).astype(BF)
    cq = _rmsnorm(_dot(xn, wdq_ref[...]), qn_ref[...]).astype(BF)
    q_nope = _dot(cq, wuqn_ref[...])
    cos = cos_ref[...]
    sin = sin_ref[...]
    q_pe = _dot(cq, wuqa_ref[...]) * cos + _dot(cq, wuqb_ref[...]) * sin
    kv = _dot(xn, wdkv_ref[...])
    c_kv = _rmsnorm(kv[:, :MLA_KV_LORA], kvn_ref[...])
    kv_ref[:, :MLA_KV_LORA] = c_kv
    kv_ref[:, MLA_KV_LORA:] = (kv[:, MLA_KV_LORA:] * cos[:, :MLA_ROPE]
                               + _dot(xn, wdkvb_ref[...]) * sin[:, :MLA_ROPE])
    if feature_major:
        qb_ref[0] = c_kv.T.astype(BF)
        q_pe_t = q_pe.T
    for h in range(MLA_HEADS):
        rows = slice(h * tq, (h + 1) * tq)
        q_lat = _dot(q_nope[:, h * MLA_NOPE:(h + 1) * MLA_NOPE], wuk_ref[h])
        if feature_major:
            qa_ref[0, :MLA_KV_LORA, rows] = q_lat.T.astype(BF)
            qa_ref[0, MLA_KV_LORA:, rows] = q_pe_t[h * MLA_ROPE:(h + 1) * MLA_ROPE, :].astype(BF)
        else:
            qa_ref[0, rows, :] = q_lat.astype(BF)
            qb_ref[0, rows, :] = q_pe[:, h * MLA_ROPE:(h + 1) * MLA_ROPE].astype(BF)


def _mla_proj(h, g, cos, sin, w, *, tq, feature_major):
    t = h.shape[0]
    nb = t // tq
    npos = cos.shape[0] // tq
    hq = MLA_HEADS * tq
    qa_shape, qb_shape = (((MLA_LAT, hq), (MLA_KV_LORA, tq)) if feature_major
                          else ((hq, MLA_KV_LORA), (hq, MLA_ROPE)))
    return pl.pallas_call(
        functools.partial(_mla_proj_kernel, feature_major=feature_major),
        grid=(nb,),
        in_specs=[
            pl.BlockSpec((tq, D_MODEL), lambda i: (i, 0)),
            _full((1, D_MODEL)),
            pl.BlockSpec((tq, MLA_HEADS * MLA_ROPE), lambda i: (i % npos, 0)),
            pl.BlockSpec((tq, MLA_HEADS * MLA_ROPE), lambda i: (i % npos, 0)),
            _full((D_MODEL, MLA_Q_LORA)),
            _full((1, MLA_Q_LORA)),
            _full((MLA_Q_LORA, MLA_HEADS * MLA_NOPE)),
            _full((MLA_Q_LORA, MLA_HEADS * MLA_ROPE)),
            _full((MLA_Q_LORA, MLA_HEADS * MLA_ROPE)),
            _full((D_MODEL, MLA_LAT)),
            _full((D_MODEL, MLA_ROPE)),
            _full((1, MLA_KV_LORA)),
            _full((MLA_HEADS, MLA_NOPE, MLA_KV_LORA)),
        ],
        out_specs=[pl.BlockSpec((tq, MLA_LAT), lambda i: (i, 0)),
                   pl.BlockSpec((1,) + qa_shape, lambda i: (i, 0, 0)),
                   pl.BlockSpec((1,) + qb_shape, lambda i: (i, 0, 0))],
        out_shape=[jax.ShapeDtypeStruct((t, MLA_LAT), F32),
                   jax.ShapeDtypeStruct((nb,) + qa_shape, BF),
                   jax.ShapeDtypeStruct((nb,) + qb_shape, BF)],
        compiler_params=_params("parallel"),
        name="mla_proj",
    )(h, g, cos, sin, w["wdq"], w["qn"], w["wuqn"], w["wuqa"], w["wuqb"], w["wdkv"],
      w["wdkvb"], w["kvn"], w["wuk"])


def _lanes(x, n):
    return x[:, :n] if n <= LANES else jnp.concatenate([x] * (n // LANES), axis=-1)


def _softmax_step(s, pv, m_scr, l_scr, acc_scr, rows):
    c2 = MLA_SCALE * LOG2E
    m_prev = m_scr[rows, :]
    m_new = jnp.maximum(m_prev, jnp.max(s, axis=-1, keepdims=True))
    alpha = jnp.exp2((m_prev - m_new) * c2)
    p = jnp.exp2((s - _lanes(m_new, s.shape[-1])) * c2)
    l_scr[rows, :] = alpha * l_scr[rows, :] + jnp.sum(p, axis=-1, keepdims=True)
    acc_scr[rows, :] = _lanes(alpha, MLA_KV_LORA) * acc_scr[rows, :] + pv(p)
    m_scr[rows, :] = m_new


def _mla_value_out(o_head, x, wuv_ref, wout_ref, ov_scr):
    for h in range(MLA_HEADS):
        ov_scr[:, h * MLA_V:(h + 1) * MLA_V] = _dot(o_head(h), wuv_ref[h]).astype(BF)
    return x + _dot(ov_scr[...], wout_ref[...])


def _mla_prompt_attn_kernel(qi_ref, kj_ref, qt_ref, kv_ref, vt_ref, x_ref, wuvt_ref, woutt_ref,
                            y_ref, m_scr, l_scr, acc_scr, ov_scr, *, tq, tk, cb):
    step_id = pl.program_id(1)
    i = qi_ref[step_id]
    j = kj_ref[step_id]
    j_diag = ((i + 1) * tq - 1) // tk
    c2 = MLA_SCALE * LOG2E

    @pl.when(j == 0)
    def _():
        m_scr[...] = jnp.full_like(m_scr, -jnp.inf)
        l_scr[...] = jnp.zeros_like(l_scr)
        acc_scr[...] = jnp.zeros_like(acc_scr)

    def step(masked):
        kb = kv_ref[...].astype(BF)
        klat, kpe = kb[:, :MLA_KV_LORA], kb[:, MLA_KV_LORA:]
        vt = vt_ref[0]
        n_blocks = MLA_HEADS * tq // cb

        def scores(c):
            cols = slice(c * cb, (c + 1) * cb)
            return (_dot(klat, qt_ref[0, :MLA_KV_LORA, cols])
                    + _dot(kpe, qt_ref[0, MLA_KV_LORA:, cols]))

        def softmax(c, s):
            cols = slice(c * cb, (c + 1) * cb)
            if masked:
                k_pos = j * tk + lax.broadcasted_iota(jnp.int32, (tk, cb), 0)
                tok = (c * cb) % tq + lax.broadcasted_iota(jnp.int32, (tk, cb), 1)
                s = jnp.where(k_pos <= i * tq + tok, s, -jnp.inf)
            m_prev = m_scr[:, cols]
            m_new = jnp.maximum(m_prev, jnp.max(s, axis=0, keepdims=True))
            alpha = jnp.exp2((m_prev - m_new) * c2)
            p = jnp.exp2((s - m_new) * c2)
            l_scr[:, cols] = alpha * l_scr[:, cols] + jnp.sum(p, axis=0, keepdims=True)
            m_scr[:, cols] = m_new
            return p.astype(BF), alpha

        def values(c, p, alpha):
            cols = slice(c * cb, (c + 1) * cb)
            acc_scr[:, cols] = alpha * acc_scr[:, cols] + _dot(vt, p)

        s_tiles, p_tiles = {}, {}
        for t in range(n_blocks + 2):
            if t < n_blocks:
                s_tiles[t] = scores(t)
            if 1 <= t <= n_blocks:
                p_tiles[t - 1] = softmax(t - 1, s_tiles.pop(t - 1))
            if t >= 2:
                values(t - 2, *p_tiles.pop(t - 2))

    @pl.when(j < j_diag)
    def _():
        step(False)

    @pl.when(j == j_diag)
    def _():
        step(True)
        for h in range(MLA_HEADS):
            cols = slice(h * tq, (h + 1) * tq)
            o_t = acc_scr[:, cols] / l_scr[:, cols]
            ov_scr[h * MLA_V:(h + 1) * MLA_V, :] = _dot(wuvt_ref[h], o_t).astype(BF)
        y_ref[...] = x_ref[...] + _dot(woutt_ref[...], ov_scr[...]).T


def _causal_schedule(nq, tq, tk):
    pairs = [(i, j) for i in range(nq) for j in range(((i + 1) * tq - 1) // tk + 1)]
    qi, kj = zip(*pairs)
    return np.asarray(qi, np.int32), np.asarray(kj, np.int32)


def _mla_prompt_attn(qt, kv, vt, h, wuvt, woutt, *, n_batch, seq, tq):
    tk = tq
    nq = seq // tq
    hq = MLA_HEADS * tq
    qi, kj = _causal_schedule(nq, tq, tk)

    def q_map(b, s, qi_ref, kj_ref):
        return (b * nq + qi_ref[s], 0, 0)

    def k_map(b, s, qi_ref, kj_ref):
        return (b * nq + kj_ref[s], 0)

    def v_map(b, s, qi_ref, kj_ref):
        return (b * nq + kj_ref[s], 0, 0)

    def tok_map(b, s, qi_ref, kj_ref):
        return (b * nq + qi_ref[s], 0)

    grid_spec = pltpu.PrefetchScalarGridSpec(
        num_scalar_prefetch=2,
        grid=(n_batch, len(qi)),
        in_specs=[
            pl.BlockSpec((1, MLA_LAT, hq), q_map),
            pl.BlockSpec((tk, MLA_LAT), k_map),
            pl.BlockSpec((1, MLA_KV_LORA, tk), v_map),
            pl.BlockSpec((tq, D_MODEL), tok_map),
            _full((MLA_HEADS, MLA_V, MLA_KV_LORA)),
            _full((D_MODEL, MLA_HEADS * MLA_V)),
        ],
        out_specs=pl.BlockSpec((tq, D_MODEL), tok_map),
        scratch_shapes=[pltpu.VMEM((1, hq), F32), pltpu.VMEM((1, hq), F32),
                        pltpu.VMEM((MLA_KV_LORA, hq), F32),
                        pltpu.VMEM((MLA_HEADS * MLA_V, tq), BF)],
    )
    return pl.pallas_call(
        functools.partial(_mla_prompt_attn_kernel, tq=tq, tk=tk, cb=256),
        grid_spec=grid_spec,
        out_shape=jax.ShapeDtypeStruct((n_batch * seq, D_MODEL), F32),
        compiler_params=_params("parallel", "arbitrary"),
        name="mla_prompt_attn",
    )(qi, kj, qt, kv, vt, h, wuvt, woutt)


def _mla_decode_attn_kernel(pt_ref, qlat_ref, qpe_ref, kvnew_ref, *rest, n_pages, n_new, group):
    del pt_ref
    page_refs = rest[:n_pages]
    o_ref, kbuf, m_scr, l_scr, acc_scr = rest[n_pages:]
    s_id = pl.program_id(1)
    qlat = qlat_ref[0]
    qpe = qpe_ref[0]
    every_row = slice(None)

    @pl.when(s_id == 0)
    def _():
        m_scr[...] = jnp.full_like(m_scr, -jnp.inf)
        l_scr[...] = jnp.zeros_like(l_scr)
        acc_scr[...] = jnp.zeros_like(acc_scr)
        kn = kvnew_ref[0].astype(BF)
        klat, kpe = kn[:, :MLA_KV_LORA], kn[:, MLA_KV_LORA:]
        s = _dot_nt(qlat, klat) + _dot_nt(qpe, kpe)
        tok = lax.broadcasted_iota(jnp.int32, s.shape, 0) & (n_new - 1)
        col = lax.broadcasted_iota(jnp.int32, s.shape, 1)
        s = jnp.where(col <= tok, s, -jnp.inf)
        _softmax_step(s, lambda p: _dot(p, klat), m_scr, l_scr, acc_scr, every_row)

    n_groups = n_pages // group
    width = group * PAGE_SIZE

    def scores(g):
        for p_id in range(g * group, (g + 1) * group):
            kbuf[:, p_id * PAGE_SIZE:(p_id + 1) * PAGE_SIZE] = page_refs[p_id][...].astype(BF)
        keys = slice(g * width, (g + 1) * width)
        return _dot(qlat, kbuf[:MLA_KV_LORA, keys]) + _dot(qpe, kbuf[MLA_KV_LORA:, keys])

    s_next = scores(0)
    for g in range(n_groups):
        s = s_next
        if g + 1 < n_groups:
            s_next = scores(g + 1)
        keys = slice(g * width, (g + 1) * width)
        _softmax_step(s, lambda p: _dot_nt(p, kbuf[:MLA_KV_LORA, keys]), m_scr, l_scr, acc_scr,
                      every_row)

    @pl.when(s_id == pl.num_programs(1) - 1)
    def _():
        o_ref[0] = acc_scr[...] / _lanes(l_scr[...], MLA_KV_LORA)


def _mla_decode_attn(page_table, qlat, qpe, kvnew, cache_t, *, layer, n_new, pages_per_step):
    n_req, n_pages_total = page_table.shape
    rows = qlat.shape[1]
    steps = n_pages_total // pages_per_step

    def page_map(p_id):
        return lambda r, s, pt: (layer, pt[r, s * pages_per_step + p_id], 0, 0)

    grid_spec = pltpu.PrefetchScalarGridSpec(
        num_scalar_prefetch=1,
        grid=(n_req, steps),
        in_specs=[
            pl.BlockSpec((1, rows, MLA_KV_LORA), lambda r, s, pt: (r, 0, 0)),
            pl.BlockSpec((1, rows, MLA_ROPE), lambda r, s, pt: (r, 0, 0)),
            pl.BlockSpec((1,) + kvnew.shape[1:], lambda r, s, pt: (r, 0, 0)),
        ] + [pl.BlockSpec((None, None, MLA_LAT, PAGE_SIZE), page_map(p_id))
             for p_id in range(pages_per_step)],
        out_specs=pl.BlockSpec((1, rows, MLA_KV_LORA), lambda r, s, pt: (r, 0, 0)),
        scratch_shapes=[pltpu.VMEM((MLA_LAT, pages_per_step * PAGE_SIZE), BF),
                        pltpu.VMEM((rows, LANES), F32), pltpu.VMEM((rows, LANES), F32),
                        pltpu.VMEM((rows, MLA_KV_LORA), F32)],
    )
    return pl.pallas_call(
        functools.partial(_mla_decode_attn_kernel, n_pages=pages_per_step, n_new=n_new,
                          group=16),
        grid_spec=grid_spec,
        out_shape=jax.ShapeDtypeStruct((n_req, rows, MLA_KV_LORA), F32),
        compiler_params=_params("parallel", "arbitrary"),
        name="mla_decode_attn",
    )(page_table, qlat, qpe, kvnew, *([cache_t] * pages_per_step))


def _mla_out_kernel(x_ref, o_ref, wuv_ref, wout_ref, y_ref, ov_scr):
    tq = x_ref.shape[0]
    y_ref[...] = _mla_value_out(lambda h: o_ref[0, h * tq:(h + 1) * tq, :], x_ref[...],
                                wuv_ref, wout_ref, ov_scr)


def _mla_out(h, o, wuv, wout, *, tq):
    t = h.shape[0]
    hq = MLA_HEADS * tq
    return pl.pallas_call(
        _mla_out_kernel,
        grid=(t // tq,),
        in_specs=[
            pl.BlockSpec((tq, D_MODEL), lambda i: (i, 0)),
            pl.BlockSpec((1, hq, MLA_KV_LORA), lambda i: (i, 0, 0)),
            _full((MLA_HEADS, MLA_KV_LORA, MLA_V)),
            _full((MLA_HEADS * MLA_V, D_MODEL)),
        ],
        out_specs=pl.BlockSpec((tq, D_MODEL), lambda i: (i, 0)),
        out_shape=jax.ShapeDtypeStruct((t, D_MODEL), F32),
        scratch_shapes=[pltpu.VMEM((tq, MLA_HEADS * MLA_V), BF)],
        compiler_params=_params("parallel"),
        name="mla_out",
    )(h, o, wuv, wout)


def _block_diag(x, w_ref, b):
    parts = [_dot(x[:, n * LRU_BLOCK_DIM:(n + 1) * LRU_BLOCK_DIM], w_ref[n])
             for n in range(LRU_BLOCKS)]
    return jnp.concatenate(parts, axis=-1) + b


def _lru_coeffs(xc, wa_ref, ba_ref, wi_ref, bi_ref, lam_ref):
    r = _sigmoid(_block_diag(xc, wa_ref, ba_ref[...]))
    gi = _sigmoid(_block_diag(xc, wi_ref, bi_ref[...]))
    neg_lam = -lam_ref[...]
    softplus = jnp.maximum(neg_lam, 0.0) + jnp.log1p(jnp.exp(-jnp.abs(neg_lam)))
    a = jnp.exp((-LRU_C * softplus) * r)
    return a, jnp.sqrt(1.0 - a * a) * gi * xc


def _scan_pitch(seg):
    pitch = seg + SUBLANES
    return pitch if (pitch // SUBLANES) % 2 else pitch + SUBLANES


def _mixer_c_prompt_kernel(x_ref, g_ref, wx_ref, wgate_ref, convw_ref, convb_ref, wa_ref, ba_ref,
                           wi_ref, bi_ref, lam_ref, wout_ref,
                           o_ref, hlast_ref, buf_ref,
                           xx_scr, gate_scr, a_scr, b_scr, hloc_scr, ploc_scr, carry_scr):
    tt = x_ref.shape[0]
    seg = tt // SUBLANES
    pitch = _scan_pitch(seg)
    ti = pl.program_id(1)

    @pl.when(ti == 0)
    def _():
        xx_scr[0:SUBLANES, :] = jnp.zeros((SUBLANES, LRU_WIDTH), F32)
        carry_scr[...] = jnp.zeros_like(carry_scr)

    x = x_ref[...]
    xn = _rmsnorm(x, g_ref[...]).astype(BF)
    gate_scr[...] = jax.nn.gelu(_dot(xn, wgate_ref[...]))
    xx_scr[SUBLANES:, :] = _dot(xn, wx_ref[...])
    xc = convb_ref[...]
    for k in range(CONV_WIDTH):
        off = SUBLANES - (CONV_WIDTH - 1) + k
        xc = xc + convw_ref[k:k + 1, :] * xx_scr[off:off + tt, :]
    buf_ref[0] = xx_scr[tt + SUBLANES - (CONV_WIDTH - 1):, :]
    xx_scr[0:SUBLANES, :] = xx_scr[tt:tt + SUBLANES, :]

    a, b = _lru_coeffs(xc, wa_ref, ba_ref, wi_ref, bi_ref, lam_ref)
    for n in range(LRU_BLOCKS):
        cols = slice(n * LRU_BLOCK_DIM, (n + 1) * LRU_BLOCK_DIM)
        for k in range(SUBLANES):
            a_scr[n, k * pitch:k * pitch + seg, :] = a[k * seg:(k + 1) * seg, cols]
            b_scr[n, k * pitch:k * pitch + seg, :] = b[k * seg:(k + 1) * seg, cols]

    def body(j, carry):
        hs, ps = carry
        rows = pl.ds(j, SUBLANES, stride=pitch)
        new_h, new_p = [], []
        for n in range(LRU_BLOCKS):
            aj = a_scr[n, rows, :]
            h = aj * hs[n] + b_scr[n, rows, :]
            p = aj * ps[n]
            hloc_scr[n, rows, :] = h
            ploc_scr[n, rows, :] = p
            new_h.append(h)
            new_p.append(p)
        return tuple(new_h), tuple(new_p)

    h_end, p_end = lax.fori_loop(
        0, seg, body,
        ((jnp.zeros((SUBLANES, LRU_BLOCK_DIM), F32),) * LRU_BLOCKS,
         (jnp.ones((SUBLANES, LRU_BLOCK_DIM), F32),) * LRU_BLOCKS))

    for n in range(LRU_BLOCKS):
        cols = slice(n * LRU_BLOCK_DIM, (n + 1) * LRU_BLOCK_DIM)
        c = carry_scr[:, cols]
        for k in range(SUBLANES):
            rows = slice(k * seg, (k + 1) * seg)
            srows = slice(k * pitch, k * pitch + seg)
            gate_scr[rows, cols] = gate_scr[rows, cols] * (hloc_scr[n, srows, :]
                                                           + ploc_scr[n, srows, :] * c)
            c = p_end[n][k:k + 1, :] * c + h_end[n][k:k + 1, :]
        carry_scr[:, cols] = c
    hlast_ref[0] = carry_scr[...]
    o_ref[...] = x + _dot(gate_scr[...], wout_ref[...])


def _mixer_c_prompt(h, g, w, *, n_batch, seq, tt):
    nt = seq // tt
    wd = LRU_WIDTH
    blk = (LRU_BLOCKS, LRU_BLOCK_DIM, LRU_BLOCK_DIM)
    return pl.pallas_call(
        _mixer_c_prompt_kernel,
        grid=(n_batch, nt),
        in_specs=[
            pl.BlockSpec((tt, D_MODEL), lambda b, i: (b * nt + i, 0)),
            _full((1, D_MODEL)),
            _full((D_MODEL, wd)), _full((D_MODEL, wd)),
            _full((CONV_WIDTH, wd)), _full((1, wd)),
            _full(blk), _full((1, wd)), _full(blk), _full((1, wd)), _full((1, wd)),
            _full((wd, D_MODEL)),
        ],
        out_specs=[pl.BlockSpec((tt, D_MODEL), lambda b, i: (b * nt + i, 0)),
                   pl.BlockSpec((1, 1, wd), lambda b, i: (b, 0, 0)),
                   pl.BlockSpec((1, CONV_WIDTH - 1, wd), lambda b, i: (b, 0, 0))],
        out_shape=[jax.ShapeDtypeStruct((n_batch * seq, D_MODEL), F32),
                   jax.ShapeDtypeStruct((n_batch, 1, wd), F32),
                   jax.ShapeDtypeStruct((n_batch, CONV_WIDTH - 1, wd), F32)],
        scratch_shapes=[pltpu.VMEM((tt + SUBLANES, wd), F32), pltpu.VMEM((tt, wd), F32)] +
                       [pltpu.VMEM((LRU_BLOCKS, SUBLANES * _scan_pitch(tt // SUBLANES),
                                    LRU_BLOCK_DIM), F32)] * 4 +
                       [pltpu.VMEM((1, wd), F32)],
        compiler_params=_params("parallel", "arbitrary"),
        name="mixer_c_prompt",
    )(h, g, w["wx"], w["wgate"], w["convw"], w["convb"], w["wa"], w["ba"], w["wi"], w["bi"],
      w["lam"], w["wout"])


def _mixer_c_sample_kernel(x_ref, g_ref, h0_ref, buf0_ref, wx_ref, wgate_ref, convw_ref, convb_ref,
                           wa_ref, ba_ref, wi_ref, bi_ref, lam_ref, wout_ref,
                           o_ref, hlast_ref, buf_ref, xx_scr, gh_scr, *, n_req):
    t = x_ref.shape[0]
    hist = (CONV_WIDTH - 1) * n_req
    x = x_ref[...]
    xn = _rmsnorm(x, g_ref[...]).astype(BF)
    gate = jax.nn.gelu(_dot(xn, wgate_ref[...]))
    xx_scr[0:hist, :] = buf0_ref[...]
    xx_scr[hist:, :] = _dot(xn, wx_ref[...])
    xc = convb_ref[...]
    for k in range(CONV_WIDTH):
        xc = xc + convw_ref[k:k + 1, :] * xx_scr[k * n_req:k * n_req + t, :]
    buf_ref[...] = xx_scr[t:, :]
    a, b = _lru_coeffs(xc, wa_ref, ba_ref, wi_ref, bi_ref, lam_ref)
    h = h0_ref[...]
    for step in range(t // n_req):
        rows = slice(step * n_req, (step + 1) * n_req)
        h = a[rows, :] * h + b[rows, :]
        gh_scr[rows, :] = gate[rows, :] * h
    hlast_ref[...] = h
    o_ref[...] = x + _dot(gh_scr[...], wout_ref[...])


def _mixer_c_sample(h, g, h0, buf0, w, *, n_req):
    t = h.shape[0]
    wd = LRU_WIDTH
    hist = (CONV_WIDTH - 1) * n_req
    blk = (LRU_BLOCKS, LRU_BLOCK_DIM, LRU_BLOCK_DIM)
    return pl.pallas_call(
        functools.partial(_mixer_c_sample_kernel, n_req=n_req),
        grid=(1,),
        in_specs=[
            _full((t, D_MODEL)), _full((1, D_MODEL)), _full((n_req, wd)), _full((hist, wd)),
            _full((D_MODEL, wd)), _full((D_MODEL, wd)),
            _full((CONV_WIDTH, wd)), _full((1, wd)),
            _full(blk), _full((1, wd)), _full(blk), _full((1, wd)), _full((1, wd)),
            _full((wd, D_MODEL)),
        ],
        out_specs=[_full((t, D_MODEL)), _full((n_req, wd)), _full((hist, wd))],
        out_shape=[jax.ShapeDtypeStruct((t, D_MODEL), F32),
                   jax.ShapeDtypeStruct((n_req, wd), F32),
                   jax.ShapeDtypeStruct((hist, wd), F32)],
        scratch_shapes=[pltpu.VMEM((hist + t, wd), F32), pltpu.VMEM((t, wd), F32)],
        compiler_params=_params("arbitrary"),
        name="mixer_c_sample",
    )(h, g, h0, buf0, w["wx"], w["wgate"], w["convw"], w["convb"], w["wa"], w["ba"], w["wi"],
      w["bi"], w["lam"], w["wout"])


def _rope_tables(pos):
    inv = 1.0 / (ROPE_THETA ** (jnp.arange(0, MLA_ROPE, 2, dtype=F32) / MLA_ROPE))
    ang = pos.astype(F32)[:, None] * inv[None, :]
    cos, sin = jnp.cos(ang), jnp.sin(ang)
    cos_t = jnp.tile(jnp.concatenate([cos, cos], axis=-1), (1, MLA_HEADS))
    sin_t = jnp.tile(jnp.concatenate([-sin, sin], axis=-1), (1, MLA_HEADS))
    return cos_t, sin_t


def _swap_halves(w):
    half = w.shape[-1] // 2
    return jnp.concatenate([w[..., half:], w[..., :half]], axis=-1)


def kernel(x_prompt, x_sample, cache_mla, state_lru_h, state_lru_conv, page_table, norm_mix, norm_ffn, norm_out, a_w_in, a_ln_g, a_ln_b, a_w_s, a_b_s, a_w_out, b_w_dq, b_q_norm, b_w_uq, b_w_dkv, b_kv_norm, b_w_uk, b_w_uv, b_w_out, c_w_x, c_w_gate, c_conv_w, c_conv_b, c_w_a, c_b_a, c_w_i, c_b_i, c_lambda, c_w_out, f_w_gate, f_w_up, f_w_down):
    n_pr, len_p, _ = x_prompt.shape
    n_dec, len_s, _ = x_sample.shape
    depth = norm_mix.shape[0]
    past_len = page_table.shape[1] * PAGE_SIZE
    t_s = n_dec * len_s
    assert len_p % A_CHUNK == 0 and len_s <= A_CHUNK and (len_s & (len_s - 1)) == 0

    def row(v):
        return v.reshape(1, -1)

    def to_tm(v):
        return jnp.swapaxes(v, 0, 1).reshape((t_s,) + v.shape[2:])

    def from_tm(v):
        return jnp.swapaxes(v.reshape((len_s, n_dec) + v.shape[1:]), 0, 1)

    hp = x_prompt.reshape(n_pr * len_p, D_MODEL)
    hs = to_tm(x_sample)

    cos_p, sin_p = _rope_tables(jnp.arange(len_p, dtype=jnp.int32))
    pos_s = past_len + jnp.arange(len_s, dtype=jnp.int32)
    cos_s, sin_s = _rope_tables(jnp.repeat(pos_s, n_dec))

    f_wg, f_wu, f_wd = f_w_gate.astype(BF), f_w_up.astype(BF), f_w_down.astype(BF)
    tril = jnp.tril(jnp.ones((A_CHUNK, A_CHUNK), dtype=bool))
    eye_req = jnp.eye(n_dec, dtype=F32)

    chunk_v_s, mla_p, mla_s = [], [], []
    lru_h_p, lru_h_s, lru_c_p, lru_c_s = [], [], [], []
    for layer in range(depth):
        kind, j = layer % 3, layer // 3
        g_mix = row(norm_mix[layer])
        if kind == 0:
            w_causal = jnp.where(tril[None], a_w_s[j], 0.0)
            w_small = w_causal[:, :len_s, :len_s]
            wmix_s = jnp.einsum("gts,rq->gtrsq", w_small, eye_req).reshape(A_GROUPS, t_s, t_s)
            pad = A_CHUNK - t_s
            wmix_s = jnp.pad(wmix_s, ((0, 0), (0, pad), (0, pad)))
            bmix_p = a_b_s[j].T
            bmix_s = jnp.pad(jnp.repeat(a_b_s[j].T[:len_s], n_dec, axis=0), ((0, pad), (0, 0)))
            args = (a_w_in[j].astype(BF), row(a_ln_g[j]), row(a_ln_b[j]))
            w_out = a_w_out[j].astype(BF)
            hp, = _mixer_a(hp, g_mix, *args, w_causal.astype(BF), bmix_p, w_out, tm=1024,
                           emit_v=False)
            hs, v_s = _mixer_a(hs, g_mix, *args, wmix_s.astype(BF), bmix_s, w_out, tm=t_s,
                               emit_v=True)
            chunk_v_s.append(from_tm(v_s))
        elif kind == 1:
            nope_cols = (jnp.arange(MLA_HEADS)[:, None] * (MLA_NOPE + MLA_ROPE)
                         + jnp.arange(MLA_NOPE)[None, :]).reshape(-1)
            pe_cols = (jnp.arange(MLA_HEADS)[:, None] * (MLA_NOPE + MLA_ROPE) + MLA_NOPE
                       + jnp.arange(MLA_ROPE)[None, :])
            pe_cols_sw = _swap_halves(pe_cols)
            w = dict(
                wdq=b_w_dq[j].astype(BF), qn=row(b_q_norm[j]),
                wuqn=b_w_uq[j][:, nope_cols].astype(BF),
                wuqa=b_w_uq[j][:, pe_cols.reshape(-1)].astype(BF),
                wuqb=b_w_uq[j][:, pe_cols_sw.reshape(-1)].astype(BF),
                wdkv=b_w_dkv[j].astype(BF),
                wdkvb=_swap_halves(b_w_dkv[j][:, MLA_KV_LORA:]).astype(BF),
                kvn=row(b_kv_norm[j]),
                wuk=jnp.transpose(b_w_uk[j], (1, 2, 0)).astype(BF),
            )
            wuv = jnp.transpose(b_w_uv[j], (1, 0, 2)).astype(BF)
            wout = b_w_out[j].astype(BF)

            tq = 512
            kv_p, qt_p, vt_p = _mla_proj(hp, g_mix, cos_p, sin_p, w, tq=tq, feature_major=True)
            wuvt = jnp.transpose(b_w_uv[j], (1, 2, 0)).astype(BF)
            hp = _mla_prompt_attn(qt_p, kv_p, vt_p, hp, wuvt, b_w_out[j].T.astype(BF),
                                  n_batch=n_pr, seq=len_p, tq=tq)
            mla_p.append(kv_p.reshape(n_pr, len_p, MLA_LAT))

            kv_s, qlat_s, qpe_s = _mla_proj(hs, g_mix, cos_s, sin_s, w, tq=t_s,
                                            feature_major=False)

            def per_request(q):
                q = q.reshape(MLA_HEADS, len_s, n_dec, q.shape[-1])
                return jnp.transpose(q, (2, 0, 1, 3)).reshape(n_dec, MLA_HEADS * len_s, q.shape[-1])

            kv_new = jnp.pad(from_tm(kv_s), ((0, 0), (0, 16 - len_s), (0, 0)))
            o_s = _mla_decode_attn(page_table, per_request(qlat_s), per_request(qpe_s), kv_new,
                                   jnp.swapaxes(cache_mla, 2, 3), layer=j, n_new=len_s,
                                   pages_per_step=64)
            o_s = o_s.reshape(n_dec, MLA_HEADS, len_s, MLA_KV_LORA)
            o_s = jnp.transpose(o_s, (1, 2, 0, 3)).reshape(1, MLA_HEADS * t_s, MLA_KV_LORA)
            hs = _mla_out(hs, o_s, wuv, wout, tq=t_s)
            mla_s.append(from_tm(kv_s))
        else:
            w = dict(
                wx=c_w_x[j].astype(BF), wgate=c_w_gate[j].astype(BF), convw=c_conv_w[j],
                convb=row(c_conv_b[j]), wa=c_w_a[j].astype(BF), ba=row(c_b_a[j]),
                wi=c_w_i[j].astype(BF), bi=row(c_b_i[j]), lam=row(c_lambda[j]),
                wout=c_w_out[j].astype(BF),
            )
            hp, hl_p, cb_p = _mixer_c_prompt(hp, g_mix, w, n_batch=n_pr, seq=len_p, tt=256)
            lru_h_p.append(hl_p.reshape(n_pr, LRU_WIDTH))
            lru_c_p.append(cb_p)
            buf0 = jnp.swapaxes(state_lru_conv[j], 0, 1).reshape(-1, LRU_WIDTH)
            hs, hl_s, cb_s = _mixer_c_sample(hs, g_mix, state_lru_h[j], buf0, w, n_req=n_dec)
            lru_h_s.append(hl_s)
            lru_c_s.append(jnp.swapaxes(cb_s.reshape(CONV_WIDTH - 1, n_dec, LRU_WIDTH), 0, 1))

        last = layer == depth - 1
        ffn_w = (row(norm_ffn[layer]), f_wg, f_wu, f_wd, row(norm_out))
        hp = _ffn(hp, *ffn_w, layer=layer, tm=1024, n_split=11, final_norm=last)
        hs = _ffn(hs, *ffn_w, layer=layer, tm=t_s, n_split=2, final_norm=last)

    return (hp.reshape(n_pr, len_p, D_MODEL), from_tm(hs), jnp.stack(chunk_v_s),
            jnp.stack(mla_p), jnp.stack(mla_s), jnp.stack(lru_h_p), jnp.stack(lru_h_s),
            jnp.stack(lru_c_p), jnp.stack(lru_c_s))
```

```python
import functools

import jax
import jax.numpy as jnp
import numpy as np
from jax import lax
from jax.experimental import pallas as pl
from jax.experimental.pallas import tpu as pltpu

BF = jnp.bfloat16
F32 = jnp.float32

D_MODEL = 1024
D_FF = 2816
A_CHUNK = 128
A_GROUPS = 8
MLA_HEADS = 8
MLA_Q_LORA = 384
MLA_KV_LORA = 256
MLA_NOPE = 128
MLA_ROPE = 64
MLA_V = 128
MLA_LAT = MLA_KV_LORA + MLA_ROPE
MLA_SCALE = (MLA_NOPE + MLA_ROPE) ** -0.5
ROPE_THETA = 10000.0
PAGE_SIZE = 128
LRU_WIDTH = 1280
LRU_BLOCKS = 10
LRU_BLOCK_DIM = 128
CONV_WIDTH = 4
LRU_C = 8.0
RMS_EPS = 1e-6
LN_EPS = 1e-5

SUBLANES = 8
LANES = 128
LOG2E = 1.4426950408889634
VMEM_LIMIT = 56 * 1024 * 1024


def _params(*sem):
    return pltpu.CompilerParams(dimension_semantics=sem, vmem_limit_bytes=VMEM_LIMIT)


def _dot(a, b):
    return jnp.dot(a.astype(BF), b.astype(BF), preferred_element_type=F32)


def _dot_nt(a, b):
    return lax.dot_general(a.astype(BF), b.astype(BF), (((1,), (1,)), ((), ())),
                           preferred_element_type=F32)


def _rmsnorm(x, g):
    return x * lax.rsqrt(jnp.mean(x * x, axis=-1, keepdims=True) + RMS_EPS) * g


def _layernorm(x, g, b):
    mu = jnp.mean(x, axis=-1, keepdims=True)
    xc = x - mu
    return xc * lax.rsqrt(jnp.mean(xc * xc, axis=-1, keepdims=True) + LN_EPS) * g + b


def _sigmoid(x):
    return 0.5 * jnp.tanh(0.5 * x) + 0.5


def _full(shape):
    return pl.BlockSpec(shape, lambda *_: (0,) * len(shape))


def _ffn_kernel(x_ref, g_ref, wg_ref, wu_ref, wd_ref, gout_ref, o_ref, h_scr, *,
                final_norm, n_split):
    x = x_ref[...]
    xn = _rmsnorm(x, g_ref[...]).astype(BF)
    width = D_FF // n_split
    for c in range(n_split):
        cols = slice(c * width, (c + 1) * width)
        gate = _dot(xn, wg_ref[:, cols])
        up = _dot(xn, wu_ref[:, cols])
        h_scr[:, cols] = (gate * _sigmoid(gate) * up).astype(BF)
    y = x + _dot(h_scr[...], wd_ref[...])
    if final_norm:
        y = _rmsnorm(y, gout_ref[...])
    o_ref[...] = y


def _ffn(h, g, wg, wu, wd, gout, *, tm, n_split, final_norm):
    t = h.shape[0]
    return pl.pallas_call(
        functools.partial(_ffn_kernel, final_norm=final_norm, n_split=n_split),
        grid=(t // tm,),
        in_specs=[
            pl.BlockSpec((tm, D_MODEL), lambda i: (i, 0)),
            _full((1, D_MODEL)),
            _full((D_MODEL, D_FF)),
            _full((D_MODEL, D_FF)),
            _full((D_FF, D_MODEL)),
            _full((1, D_MODEL)),
        ],
        out_specs=pl.BlockSpec((tm, D_MODEL), lambda i: (i, 0)),
        out_shape=jax.ShapeDtypeStruct((t, D_MODEL), F32),
        scratch_shapes=[pltpu.VMEM((tm, D_FF), BF)],
        compiler_params=_params("parallel"),
        name="ffn",
    )(h, g, wg, wu, wd, gout)


def _ffn_stream_kernel(x_ref, g_ref, wg_ref, wu_ref, wd_ref, gout_ref,
                       o_ref, wg16_ref, wu16_ref, wd16_ref, xn_scr, acc_scr, *, final_norm):
    j = pl.program_id(0)

    @pl.when(j == 0)
    def _():
        xn_scr[...] = _rmsnorm(x_ref[...], g_ref[...]).astype(BF)
        acc_scr[...] = jnp.zeros_like(acc_scr)

    wg, wu, wd = wg_ref[...].astype(BF), wu_ref[...].astype(BF), wd_ref[...].astype(BF)
    wg16_ref[...] = wg
    wu16_ref[...] = wu
    wd16_ref[...] = wd
    xn = xn_scr[...]
    gate = _dot(xn, wg)
    acc_scr[...] += _dot(gate * _sigmoid(gate) * _dot(xn, wu), wd)

    @pl.when(j == pl.num_programs(0) - 1)
    def _():
        y = x_ref[...] + acc_scr[...]
        if final_norm:
            y = _rmsnorm(y, gout_ref[...])
        o_ref[...] = y


def _ffn_stream(h, g, wg, wu, wd, gout, *, layer, tf, final_norm):
    t = h.shape[0]
    return pl.pallas_call(
        functools.partial(_ffn_stream_kernel, final_norm=final_norm),
        grid=(D_FF // tf,),
        in_specs=[
            _full((t, D_MODEL)),
            _full((1, D_MODEL)),
            pl.BlockSpec((None, D_MODEL, tf), lambda j: (layer, 0, j)),
            pl.BlockSpec((None, D_MODEL, tf), lambda j: (layer, 0, j)),
            pl.BlockSpec((None, tf, D_MODEL), lambda j: (layer, j, 0)),
            _full((1, D_MODEL)),
        ],
        out_specs=[_full((t, D_MODEL)),
                   pl.BlockSpec((D_MODEL, tf), lambda j: (0, j)),
                   pl.BlockSpec((D_MODEL, tf), lambda j: (0, j)),
                   pl.BlockSpec((tf, D_MODEL), lambda j: (j, 0))],
        out_shape=[jax.ShapeDtypeStruct((t, D_MODEL), F32),
                   jax.ShapeDtypeStruct((D_MODEL, D_FF), BF),
                   jax.ShapeDtypeStruct((D_MODEL, D_FF), BF),
                   jax.ShapeDtypeStruct((D_FF, D_MODEL), BF)],
        scratch_shapes=[pltpu.VMEM((t, D_MODEL), BF), pltpu.VMEM((t, D_MODEL), F32)],
        compiler_params=_params("arbitrary"),
        name="ffn_stream",
    )(h, g, wg, wu, wd, gout)


def _mixer_a_kernel(x_ref, g_ref, win_ref, lng_ref, lnb_ref, wmix_ref, bmix_ref, wout_ref,
                    o_ref, *rest, emit_v, n_sub):
    u_scr, vb_scr, uv_scr = rest[-3:]
    sub = x_ref.shape[0] // n_sub

    def tile(part):
        return slice(part * sub, (part + 1) * sub)

    def project_in(part):
        xn = _rmsnorm(x_ref[tile(part), :], g_ref[...]).astype(BF)
        u_scr[tile(part), :] = jax.nn.gelu(_dot(xn, win_ref[:, :D_MODEL]))
        v = _layernorm(jax.nn.gelu(_dot(xn, win_ref[:, D_MODEL:])), lng_ref[...], lnb_ref[...])
        if emit_v:
            rest[0][tile(part), :] = v
        vb_scr[tile(part), :] = v.astype(BF)

    def mix(part):
        for c in range(part * sub // A_CHUNK, (part + 1) * sub // A_CHUNK):
            rows = slice(c * A_CHUNK, (c + 1) * A_CHUNK)
            for g in range(A_GROUPS):
                cols = slice(g * A_CHUNK, (g + 1) * A_CHUNK)
                mixed = _dot(wmix_ref[g], vb_scr[rows, cols]) + bmix_ref[:, g:g + 1]
                uv_scr[rows, cols] = (u_scr[rows, cols] * mixed).astype(BF)

    def project_out(part):
        o_ref[tile(part), :] = x_ref[tile(part), :] + _dot(uv_scr[tile(part), :], wout_ref[...])

    for t in range(n_sub + 2):
        if t < n_sub:
            project_in(t)
        if 1 <= t <= n_sub:
            mix(t - 1)
        if t >= 2:
            project_out(t - 2)


def _mixer_a(h, g, win, lng, lnb, wmix, bmix, wout, *, tm, emit_v):
    t = h.shape[0]
    n_out = 2 if emit_v else 1
    return pl.pallas_call(
        functools.partial(_mixer_a_kernel, emit_v=emit_v, n_sub=max(1, tm // (2 * A_CHUNK))),
        grid=(t // tm,),
        in_specs=[
            pl.BlockSpec((tm, D_MODEL), lambda i: (i, 0)),
            _full((1, D_MODEL)),
            _full((D_MODEL, 2 * D_MODEL)),
            _full((1, D_MODEL)),
            _full((1, D_MODEL)),
            _full((A_GROUPS, A_CHUNK, A_CHUNK)),
            _full((A_CHUNK, A_GROUPS)),
            _full((D_MODEL, D_MODEL)),
        ],
        out_specs=[pl.BlockSpec((tm, D_MODEL), lambda i: (i, 0))] * n_out,
        out_shape=[jax.ShapeDtypeStruct((t, D_MODEL), F32)] * n_out,
        scratch_shapes=[pltpu.VMEM((tm, D_MODEL), F32), pltpu.VMEM((tm, D_MODEL), BF),
                        pltpu.VMEM((tm, D_MODEL), BF)],
        compiler_params=_params("parallel"),
        name="mixer_a",
    )(h, g, win, lng, lnb, wmix, bmix, wout)


def _mla_proj_kernel(x_ref, g_ref, cos_ref, sin_ref, wdq_ref, qn_ref, wuqn_ref, wuqa_ref,
                     wuqb_ref, wdkv_ref, wdkvb_ref, kvn_ref, wuk_ref,
                     kv_ref, qa_ref, qb_ref, *, feature_major):
    tq = x_ref.shape[0]
    xn = _rmsnorm(x_ref[...], g_ref[...]).astype(BF)
    cq = _rmsnorm(_dot(xn, wdq_ref[...]), qn_ref[...]).astype(BF)
    q_nope = _dot(cq, wuqn_ref[...])
    cos = cos_ref[...]
    sin = sin_ref[...]
    q_pe = _dot(cq, wuqa_ref[...]) * cos + _dot(cq, wuqb_ref[...]) * sin
    kv = _dot(xn, wdkv_ref[...])
    c_kv = _rmsnorm(kv[:, :MLA_KV_LORA], kvn_ref[...])
    kv_ref[:, :MLA_KV_LORA] = c_kv
    kv_ref[:, MLA_KV_LORA:] = (kv[:, MLA_KV_LORA:] * cos[:, :MLA_ROPE]
                               + _dot(xn, wdkvb_ref[...]) * sin[:, :MLA_ROPE])
    if feature_major:
        qb_ref[0] = c_kv.T.astype(BF)
        q_pe_t = q_pe.T
    for h in range(MLA_HEADS):
        rows = slice(h * tq, (h + 1) * tq)
        q_lat = _dot(q_nope[:, h * MLA_NOPE:(h + 1) * MLA_NOPE], wuk_ref[h])
        if feature_major:
            qa_ref[0, :MLA_KV_LORA, rows] = q_lat.T.astype(BF)
            qa_ref[0, MLA_KV_LORA:, rows] = q_pe_t[h * MLA_ROPE:(h + 1) * MLA_ROPE, :].astype(BF)
        else:
            qa_ref[0, rows, :] = q_lat.astype(BF)
            qb_ref[0, rows, :] = q_pe[:, h * MLA_ROPE:(h + 1) * MLA_ROPE].astype(BF)


def _mla_proj(h, g, cos, sin, w, *, tq, feature_major):
    t = h.shape[0]
    nb = t // tq
    npos = cos.shape[0] // tq
    hq = MLA_HEADS * tq
    qa_shape, qb_shape = (((MLA_LAT, hq), (MLA_KV_LORA, tq)) if feature_major
                          else ((hq, MLA_KV_LORA), (hq, MLA_ROPE)))
    return pl.pallas_call(
        functools.partial(_mla_proj_kernel, feature_major=feature_major),
        grid=(nb,),
        in_specs=[
            pl.BlockSpec((tq, D_MODEL), lambda i: (i, 0)),
            _full((1, D_MODEL)),
            pl.BlockSpec((tq, MLA_HEADS * MLA_ROPE), lambda i: (i % npos, 0)),
            pl.BlockSpec((tq, MLA_HEADS * MLA_ROPE), lambda i: (i % npos, 0)),
            _full((D_MODEL, MLA_Q_LORA)),
            _full((1, MLA_Q_LORA)),
            _full((MLA_Q_LORA, MLA_HEADS * MLA_NOPE)),
            _full((MLA_Q_LORA, MLA_HEADS * MLA_ROPE)),
            _full((MLA_Q_LORA, MLA_HEADS * MLA_ROPE)),
            _full((D_MODEL, MLA_LAT)),
            _full((D_MODEL, MLA_ROPE)),
            _full((1, MLA_KV_LORA)),
            _full((MLA_HEADS, MLA_NOPE, MLA_KV_LORA)),
        ],
        out_specs=[pl.BlockSpec((tq, MLA_LAT), lambda i: (i, 0)),
                   pl.BlockSpec((1,) + qa_shape, lambda i: (i, 0, 0)),
                   pl.BlockSpec((1,) + qb_shape, lambda i: (i, 0, 0))],
        out_shape=[jax.ShapeDtypeStruct((t, MLA_LAT), F32),
                   jax.ShapeDtypeStruct((nb,) + qa_shape, BF),
                   jax.ShapeDtypeStruct((nb,) + qb_shape, BF)],
        compiler_params=_params("parallel"),
        name="mla_proj",
    )(h, g, cos, sin, w["wdq"], w["qn"], w["wuqn"], w["wuqa"], w["wuqb"], w["wdkv"],
      w["wdkvb"], w["kvn"], w["wuk"])


def _lanes(x, n):
    return x[:, :n] if n <= LANES else jnp.concatenate([x] * (n // LANES), axis=-1)


def _softmax_step(s, pv, m_scr, l_scr, acc_scr, rows):
    c2 = MLA_SCALE * LOG2E
    m_prev = m_scr[rows, :]
    m_new = jnp.maximum(m_prev, jnp.max(s, axis=-1, keepdims=True))
    alpha = jnp.exp2((m_prev - m_new) * c2)
    p = jnp.exp2((s - _lanes(m_new, s.shape[-1])) * c2)
    l_scr[rows, :] = alpha * l_scr[rows, :] + jnp.sum(p, axis=-1, keepdims=True)
    acc_scr[rows, :] = _lanes(alpha, MLA_KV_LORA) * acc_scr[rows, :] + pv(p)
    m_scr[rows, :] = m_new


def _mla_value_out(o_head, x, wuv_ref, wout_ref, ov_scr):
    for h in range(MLA_HEADS):
        ov_scr[:, h * MLA_V:(h + 1) * MLA_V] = _dot(o_head(h), wuv_ref[h]).astype(BF)
    return x + _dot(ov_scr[...], wout_ref[...])


def _mla_prompt_attn_kernel(qi_ref, kj_ref, qt_ref, kv_ref, vt_ref, x_ref, wuvt_ref, woutt_ref,
                            y_ref, m_scr, l_scr, acc_scr, ov_scr, *, tq, tk, cb):
    step_id = pl.program_id(1)
    i = qi_ref[step_id]
    j = kj_ref[step_id]
    j_diag = ((i + 1) * tq - 1) // tk
    c2 = MLA_SCALE * LOG2E

    @pl.when(j == 0)
    def _():
        m_scr[...] = jnp.full_like(m_scr, -jnp.inf)
        l_scr[...] = jnp.zeros_like(l_scr)
        acc_scr[...] = jnp.zeros_like(acc_scr)

    def step(masked):
        kb = kv_ref[...].astype(BF)
        klat, kpe = kb[:, :MLA_KV_LORA], kb[:, MLA_KV_LORA:]
        vt = vt_ref[0]
        n_blocks = MLA_HEADS * tq // cb

        def scores(c):
            cols = slice(c * cb, (c + 1) * cb)
            return (_dot(klat, qt_ref[0, :MLA_KV_LORA, cols])
                    + _dot(kpe, qt_ref[0, MLA_KV_LORA:, cols]))

        def softmax(c, s):
            cols = slice(c * cb, (c + 1) * cb)
            if masked:
                k_pos = j * tk + lax.broadcasted_iota(jnp.int32, (tk, cb), 0)
                tok = (c * cb) % tq + lax.broadcasted_iota(jnp.int32, (tk, cb), 1)
                s = jnp.where(k_pos <= i * tq + tok, s, -jnp.inf)
            m_prev = m_scr[:, cols]
            m_new = jnp.maximum(m_prev, jnp.max(s, axis=0, keepdims=True))
            alpha = jnp.exp2((m_prev - m_new) * c2)
            p = jnp.exp2((s - m_new) * c2)
            l_scr[:, cols] = alpha * l_scr[:, cols] + jnp.sum(p, axis=0, keepdims=True)
            m_scr[:, cols] = m_new
            return p.astype(BF), alpha

        def values(c, p, alpha):
            cols = slice(c * cb, (c + 1) * cb)
            acc_scr[:, cols] = alpha * acc_scr[:, cols] + _dot(vt, p)

        s_tiles, p_tiles = {}, {}
        for t in range(n_blocks + 2):
            if t < n_blocks:
                s_tiles[t] = scores(t)
            if 1 <= t <= n_blocks:
                p_tiles[t - 1] = softmax(t - 1, s_tiles.pop(t - 1))
            if t >= 2:
                values(t - 2, *p_tiles.pop(t - 2))

    @pl.when(j < j_diag)
    def _():
        step(False)

    @pl.when(j == j_diag)
    def _():
        step(True)
        for h in range(MLA_HEADS):
            cols = slice(h * tq, (h + 1) * tq)
            o_t = acc_scr[:, cols] / l_scr[:, cols]
            ov_scr[h * MLA_V:(h + 1) * MLA_V, :] = _dot(wuvt_ref[h], o_t).astype(BF)
        y_ref[...] = x_ref[...] + _dot(woutt_ref[...], ov_scr[...]).T


def _causal_schedule(nq, tq, tk):
    pairs = [(i, j) for i in range(nq) for j in range(((i + 1) * tq - 1) // tk + 1)]
    qi, kj = zip(*pairs)
    return np.asarray(qi, np.int32), np.asarray(kj, np.int32)


def _mla_prompt_attn(qt, kv, vt, h, wuvt, woutt, *, n_batch, seq, tq):
    tk = tq
    nq = seq // tq
    hq = MLA_HEADS * tq
    qi, kj = _causal_schedule(nq, tq, tk)

    def q_map(b, s, qi_ref, kj_ref):
        return (b * nq + qi_ref[s], 0, 0)

    def k_map(b, s, qi_ref, kj_ref):
        return (b * nq + kj_ref[s], 0)

    def v_map(b, s, qi_ref, kj_ref):
        return (b * nq + kj_ref[s], 0, 0)

    def tok_map(b, s, qi_ref, kj_ref):
        return (b * nq + qi_ref[s], 0)

    grid_spec = pltpu.PrefetchScalarGridSpec(
        num_scalar_prefetch=2,
        grid=(n_batch, len(qi)),
        in_specs=[
            pl.BlockSpec((1, MLA_LAT, hq), q_map),
            pl.BlockSpec((tk, MLA_LAT), k_map),
            pl.BlockSpec((1, MLA_KV_LORA, tk), v_map),
            pl.BlockSpec((tq, D_MODEL), tok_map),
            _full((MLA_HEADS, MLA_V, MLA_KV_LORA)),
            _full((D_MODEL, MLA_HEADS * MLA_V)),
        ],
        out_specs=pl.BlockSpec((tq, D_MODEL), tok_map),
        scratch_shapes=[pltpu.VMEM((1, hq), F32), pltpu.VMEM((1, hq), F32),
                        pltpu.VMEM((MLA_KV_LORA, hq), F32),
                        pltpu.VMEM((MLA_HEADS * MLA_V, tq), BF)],
    )
    return pl.pallas_call(
        functools.partial(_mla_prompt_attn_kernel, tq=tq, tk=tk, cb=256),
        grid_spec=grid_spec,
        out_shape=jax.ShapeDtypeStruct((n_batch * seq, D_MODEL), F32),
        compiler_params=_params("parallel", "arbitrary"),
        name="mla_prompt_attn",
    )(qi, kj, qt, kv, vt, h, wuvt, woutt)


def _mla_decode_attn_kernel(pt_ref, qlat_ref, qpe_ref, kvnew_ref, *rest, n_pages, n_new, group):
    del pt_ref
    page_refs = rest[:n_pages]
    o_ref, kbuf, m_scr, l_scr, acc_scr = rest[n_pages:]
    s_id = pl.program_id(1)
    qlat = qlat_ref[0]
    qpe = qpe_ref[0]
    every_row = slice(None)

    @pl.when(s_id == 0)
    def _():
        m_scr[...] = jnp.full_like(m_scr, -jnp.inf)
        l_scr[...] = jnp.zeros_like(l_scr)
        acc_scr[...] = jnp.zeros_like(acc_scr)
        kn = kvnew_ref[0].astype(BF)
        klat, kpe = kn[:, :MLA_KV_LORA], kn[:, MLA_KV_LORA:]
        s = _dot_nt(qlat, klat) + _dot_nt(qpe, kpe)
        tok = lax.broadcasted_iota(jnp.int32, s.shape, 0) & (n_new - 1)
        col = lax.broadcasted_iota(jnp.int32, s.shape, 1)
        s = jnp.where(col <= tok, s, -jnp.inf)
        _softmax_step(s, lambda p: _dot(p, klat), m_scr, l_scr, acc_scr, every_row)

    n_groups = n_pages // group
    width = group * PAGE_SIZE

    def scores(g):
        for p_id in range(g * group, (g + 1) * group):
            kbuf[:, p_id * PAGE_SIZE:(p_id + 1) * PAGE_SIZE] = page_refs[p_id][...].astype(BF)
        keys = slice(g * width, (g + 1) * width)
        return _dot(qlat, kbuf[:MLA_KV_LORA, keys]) + _dot(qpe, kbuf[MLA_KV_LORA:, keys])

    s_next = scores(0)
    for g in range(n_groups):
        s = s_next
        if g + 1 < n_groups:
            s_next = scores(g + 1)
        keys = slice(g * width, (g + 1) * width)
        _softmax_step(s, lambda p: _dot_nt(p, kbuf[:MLA_KV_LORA, keys]), m_scr, l_scr, acc_scr,
                      every_row)

    @pl.when(s_id == pl.num_programs(1) - 1)
    def _():
        o_ref[0] = acc_scr[...] / _lanes(l_scr[...], MLA_KV_LORA)


def _mla_decode_attn(page_table, qlat, qpe, kvnew, cache_t, *, layer, n_new, pages_per_step):
    n_req, n_pages_total = page_table.shape
    rows = qlat.shape[1]
    steps = n_pages_total // pages_per_step

    def page_map(p_id):
        return lambda r, s, pt: (layer, pt[r, s * pages_per_step + p_id], 0, 0)

    grid_spec = pltpu.PrefetchScalarGridSpec(
        num_scalar_prefetch=1,
        grid=(n_req, steps),
        in_specs=[
            pl.BlockSpec((1, rows, MLA_KV_LORA), lambda r, s, pt: (r, 0, 0)),
            pl.BlockSpec((1, rows, MLA_ROPE), lambda r, s, pt: (r, 0, 0)),
            pl.BlockSpec((1,) + kvnew.shape[1:], lambda r, s, pt: (r, 0, 0)),
        ] + [pl.BlockSpec((None, None, MLA_LAT, PAGE_SIZE), page_map(p_id))
             for p_id in range(pages_per_step)],
        out_specs=pl.BlockSpec((1, rows, MLA_KV_LORA), lambda r, s, pt: (r, 0, 0)),
        scratch_shapes=[pltpu.VMEM((MLA_LAT, pages_per_step * PAGE_SIZE), BF),
                        pltpu.VMEM((rows, LANES), F32), pltpu.VMEM((rows, LANES), F32),
                        pltpu.VMEM((rows, MLA_KV_LORA), F32)],
    )
    return pl.pallas_call(
        functools.partial(_mla_decode_attn_kernel, n_pages=pages_per_step, n_new=n_new,
                          group=16),
        grid_spec=grid_spec,
        out_shape=jax.ShapeDtypeStruct((n_req, rows, MLA_KV_LORA), F32),
        compiler_params=_params("parallel", "arbitrary"),
        name="mla_decode_attn",
    )(page_table, qlat, qpe, kvnew, *([cache_t] * pages_per_step))


def _mla_out_kernel(x_ref, o_ref, wuv_ref, wout_ref, y_ref, ov_scr):
    tq = x_ref.shape[0]
    y_ref[...] = _mla_value_out(lambda h: o_ref[0, h * tq:(h + 1) * tq, :], x_ref[...],
                                wuv_ref, wout_ref, ov_scr)


def _mla_out(h, o, wuv, wout, *, tq):
    t = h.shape[0]
    hq = MLA_HEADS * tq
    return pl.pallas_call(
        _mla_out_kernel,
        grid=(t // tq,),
        in_specs=[
            pl.BlockSpec((tq, D_MODEL), lambda i: (i, 0)),
            pl.BlockSpec((1, hq, MLA_KV_LORA), lambda i: (i, 0, 0)),
            _full((MLA_HEADS, MLA_KV_LORA, MLA_V)),
            _full((MLA_HEADS * MLA_V, D_MODEL)),
        ],
        out_specs=pl.BlockSpec((tq, D_MODEL), lambda i: (i, 0)),
        out_shape=jax.ShapeDtypeStruct((t, D_MODEL), F32),
        scratch_shapes=[pltpu.VMEM((tq, MLA_HEADS * MLA_V), BF)],
        compiler_params=_params("parallel"),
        name="mla_out",
    )(h, o, wuv, wout)


def _block_diag(x, w_ref, b):
    parts = [_dot(x[:, n * LRU_BLOCK_DIM:(n + 1) * LRU_BLOCK_DIM], w_ref[n])
             for n in range(LRU_BLOCKS)]
    return jnp.concatenate(parts, axis=-1) + b


def _lru_coeffs(xc, wa_ref, ba_ref, wi_ref, bi_ref, lam_ref):
    r = _sigmoid(_block_diag(xc, wa_ref, ba_ref[...]))
    gi = _sigmoid(_block_diag(xc, wi_ref, bi_ref[...]))
    neg_lam = -lam_ref[...]
    softplus = jnp.maximum(neg_lam, 0.0) + jnp.log1p(jnp.exp(-jnp.abs(neg_lam)))
    a = jnp.exp((-LRU_C * softplus) * r)
    return a, jnp.sqrt(1.0 - a * a) * gi * xc


def _scan_pitch(seg):
    pitch = seg + SUBLANES
    return pitch if (pitch // SUBLANES) % 2 else pitch + SUBLANES


def _mixer_c_prompt_kernel(x_ref, g_ref, wx_ref, wgate_ref, convw_ref, convb_ref, wa_ref, ba_ref,
                           wi_ref, bi_ref, lam_ref, wout_ref,
                           o_ref, hlast_ref, buf_ref,
                           xx_scr, gate_scr, a_scr, b_scr, hloc_scr, ploc_scr, carry_scr):
    tt = x_ref.shape[0]
    seg = tt // SUBLANES
    pitch = _scan_pitch(seg)
    ti = pl.program_id(1)

    @pl.when(ti == 0)
    def _():
        xx_scr[0:SUBLANES, :] = jnp.zeros((SUBLANES, LRU_WIDTH), F32)
        carry_scr[...] = jnp.zeros_like(carry_scr)

    x = x_ref[...]
    xn = _rmsnorm(x, g_ref[...]).astype(BF)
    gate_scr[...] = jax.nn.gelu(_dot(xn, wgate_ref[...]))
    xx_scr[SUBLANES:, :] = _dot(xn, wx_ref[...])
    xc = convb_ref[...]
    for k in range(CONV_WIDTH):
        off = SUBLANES - (CONV_WIDTH - 1) + k
        xc = xc + convw_ref[k:k + 1, :] * xx_scr[off:off + tt, :]
    buf_ref[0] = xx_scr[tt + SUBLANES - (CONV_WIDTH - 1):, :]
    xx_scr[0:SUBLANES, :] = xx_scr[tt:tt + SUBLANES, :]

    a, b = _lru_coeffs(xc, wa_ref, ba_ref, wi_ref, bi_ref, lam_ref)
    for n in range(LRU_BLOCKS):
        cols = slice(n * LRU_BLOCK_DIM, (n + 1) * LRU_BLOCK_DIM)
        for k in range(SUBLANES):
            a_scr[n, k * pitch:k * pitch + seg, :] = a[k * seg:(k + 1) * seg, cols]
            b_scr[n, k * pitch:k * pitch + seg, :] = b[k * seg:(k + 1) * seg, cols]

    def body(j, carry):
        hs, ps = carry
        rows = pl.ds(j, SUBLANES, stride=pitch)
        new_h, new_p = [], []
        for n in range(LRU_BLOCKS):
            aj = a_scr[n, rows, :]
            h = aj * hs[n] + b_scr[n, rows, :]
            p = aj * ps[n]
            hloc_scr[n, rows, :] = h
            ploc_scr[n, rows, :] = p
            new_h.append(h)
            new_p.append(p)
        return tuple(new_h), tuple(new_p)

    h_end, p_end = lax.fori_loop(
        0, seg, body,
        ((jnp.zeros((SUBLANES, LRU_BLOCK_DIM), F32),) * LRU_BLOCKS,
         (jnp.ones((SUBLANES, LRU_BLOCK_DIM), F32),) * LRU_BLOCKS))

    for n in range(LRU_BLOCKS):
        cols = slice(n * LRU_BLOCK_DIM, (n + 1) * LRU_BLOCK_DIM)
        c = carry_scr[:, cols]
        for k in range(SUBLANES):
            rows = slice(k * seg, (k + 1) * seg)
            srows = slice(k * pitch, k * pitch + seg)
            gate_scr[rows, cols] = gate_scr[rows, cols] * (hloc_scr[n, srows, :]
                                                           + ploc_scr[n, srows, :] * c)
            c = p_end[n][k:k + 1, :] * c + h_end[n][k:k + 1, :]
        carry_scr[:, cols] = c
    hlast_ref[0] = carry_scr[...]
    o_ref[...] = x + _dot(gate_scr[...], wout_ref[...])


def _mixer_c_prompt(h, g, w, *, n_batch, seq, tt):
    nt = seq // tt
    wd = LRU_WIDTH
    blk = (LRU_BLOCKS, LRU_BLOCK_DIM, LRU_BLOCK_DIM)
    return pl.pallas_call(
        _mixer_c_prompt_kernel,
        grid=(n_batch, nt),
        in_specs=[
            pl.BlockSpec((tt, D_MODEL), lambda b, i: (b * nt + i, 0)),
            _full((1, D_MODEL)),
            _full((D_MODEL, wd)), _full((D_MODEL, wd)),
            _full((CONV_WIDTH, wd)), _full((1, wd)),
            _full(blk), _full((1, wd)), _full(blk), _full((1, wd)), _full((1, wd)),
            _full((wd, D_MODEL)),
        ],
        out_specs=[pl.BlockSpec((tt, D_MODEL), lambda b, i: (b * nt + i, 0)),
                   pl.BlockSpec((1, 1, wd), lambda b, i: (b, 0, 0)),
                   pl.BlockSpec((1, CONV_WIDTH - 1, wd), lambda b, i: (b, 0, 0))],
        out_shape=[jax.ShapeDtypeStruct((n_batch * seq, D_MODEL), F32),
                   jax.ShapeDtypeStruct((n_batch, 1, wd), F32),
                   jax.ShapeDtypeStruct((n_batch, CONV_WIDTH - 1, wd), F32)],
        scratch_shapes=[pltpu.VMEM((tt + SUBLANES, wd), F32), pltpu.VMEM((tt, wd), F32)] +
                       [pltpu.VMEM((LRU_BLOCKS, SUBLANES * _scan_pitch(tt // SUBLANES),
                                    LRU_BLOCK_DIM), F32)] * 4 +
                       [pltpu.VMEM((1, wd), F32)],
        compiler_params=_params("parallel", "arbitrary"),
        name="mixer_c_prompt",
    )(h, g, w["wx"], w["wgate"], w["convw"], w["convb"], w["wa"], w["ba"], w["wi"], w["bi"],
      w["lam"], w["wout"])


def _mixer_c_sample_kernel(x_ref, g_ref, h0_ref, buf0_ref, wx_ref, wgate_ref, convw_ref, convb_ref,
                           wa_ref, ba_ref, wi_ref, bi_ref, lam_ref, wout_ref,
                           o_ref, hlast_ref, buf_ref, xx_scr, gh_scr, *, n_req):
    t = x_ref.shape[0]
    hist = (CONV_WIDTH - 1) * n_req
    x = x_ref[...]
    xn = _rmsnorm(x, g_ref[...]).astype(BF)
    gate = jax.nn.gelu(_dot(xn, wgate_ref[...]))
    xx_scr[0:hist, :] = buf0_ref[...]
    xx_scr[hist:, :] = _dot(xn, wx_ref[...])
    xc = convb_ref[...]
    for k in range(CONV_WIDTH):
        xc = xc + convw_ref[k:k + 1, :] * xx_scr[k * n_req:k * n_req + t, :]
    buf_ref[...] = xx_scr[t:, :]
    a, b = _lru_coeffs(xc, wa_ref, ba_ref, wi_ref, bi_ref, lam_ref)
    h = h0_ref[...]
    for step in range(t // n_req):
        rows = slice(step * n_req, (step + 1) * n_req)
        h = a[rows, :] * h + b[rows, :]
        gh_scr[rows, :] = gate[rows, :] * h
    hlast_ref[...] = h
    o_ref[...] = x + _dot(gh_scr[...], wout_ref[...])


def _mixer_c_sample(h, g, h0, buf0, w, *, n_req):
    t = h.shape[0]
    wd = LRU_WIDTH
    hist = (CONV_WIDTH - 1) * n_req
    blk = (LRU_BLOCKS, LRU_BLOCK_DIM, LRU_BLOCK_DIM)
    return pl.pallas_call(
        functools.partial(_mixer_c_sample_kernel, n_req=n_req),
        grid=(1,),
        in_specs=[
            _full((t, D_MODEL)), _full((1, D_MODEL)), _full((n_req, wd)), _full((hist, wd)),
            _full((D_MODEL, wd)), _full((D_MODEL, wd)),
            _full((CONV_WIDTH, wd)), _full((1, wd)),
            _full(blk), _full((1, wd)), _full(blk), _full((1, wd)), _full((1, wd)),
            _full((wd, D_MODEL)),
        ],
        out_specs=[_full((t, D_MODEL)), _full((n_req, wd)), _full((hist, wd))],
        out_shape=[jax.ShapeDtypeStruct((t, D_MODEL), F32),
                   jax.ShapeDtypeStruct((n_req, wd), F32),
                   jax.ShapeDtypeStruct((hist, wd), F32)],
        scratch_shapes=[pltpu.VMEM((hist + t, wd), F32), pltpu.VMEM((t, wd), F32)],
        compiler_params=_params("arbitrary"),
        name="mixer_c_sample",
    )(h, g, h0, buf0, w["wx"], w["wgate"], w["convw"], w["convb"], w["wa"], w["ba"], w["wi"],
      w["bi"], w["lam"], w["wout"])


def _rope_tables(pos):
    inv = 1.0 / (ROPE_THETA ** (jnp.arange(0, MLA_ROPE, 2, dtype=F32) / MLA_ROPE))
    ang = pos.astype(F32)[:, None] * inv[None, :]
    cos, sin = jnp.cos(ang), jnp.sin(ang)
    cos_t = jnp.tile(jnp.concatenate([cos, cos], axis=-1), (1, MLA_HEADS))
    sin_t = jnp.tile(jnp.concatenate([-sin, sin], axis=-1), (1, MLA_HEADS))
    return cos_t, sin_t


def _swap_halves(w):
    half = w.shape[-1] // 2
    return jnp.concatenate([w[..., half:], w[..., :half]], axis=-1)


def kernel(x_prompt, x_sample, cache_mla, state_lru_h, state_lru_conv, page_table, norm_mix, norm_ffn, norm_out, a_w_in, a_ln_g, a_ln_b, a_w_s, a_b_s, a_w_out, b_w_dq, b_q_norm, b_w_uq, b_w_dkv, b_kv_norm, b_w_uk, b_w_uv, b_w_out, c_w_x, c_w_gate, c_conv_w, c_conv_b, c_w_a, c_b_a, c_w_i, c_b_i, c_lambda, c_w_out, f_w_gate, f_w_up, f_w_down):
    n_pr, len_p, _ = x_prompt.shape
    n_dec, len_s, _ = x_sample.shape
    depth = norm_mix.shape[0]
    past_len = page_table.shape[1] * PAGE_SIZE
    t_s = n_dec * len_s
    assert len_p % A_CHUNK == 0 and len_s <= A_CHUNK and (len_s & (len_s - 1)) == 0

    def row(v):
        return v.reshape(1, -1)

    def to_tm(v):
        return jnp.swapaxes(v, 0, 1).reshape((t_s,) + v.shape[2:])

    def from_tm(v):
        return jnp.swapaxes(v.reshape((len_s, n_dec) + v.shape[1:]), 0, 1)

    hp = x_prompt.reshape(n_pr * len_p, D_MODEL)
    hs = to_tm(x_sample)

    cos_p, sin_p = _rope_tables(jnp.arange(len_p, dtype=jnp.int32))
    pos_s = past_len + jnp.arange(len_s, dtype=jnp.int32)
    cos_s, sin_s = _rope_tables(jnp.repeat(pos_s, n_dec))

    tril = jnp.tril(jnp.ones((A_CHUNK, A_CHUNK), dtype=bool))
    eye_req = jnp.eye(n_dec, dtype=F32)

    chunk_v_s, mla_p, mla_s = [], [], []
    lru_h_p, lru_h_s, lru_c_p, lru_c_s = [], [], [], []
    for layer in range(depth):
        kind, j = layer % 3, layer // 3
        g_mix = row(norm_mix[layer])
        if kind == 0:
            w_causal = jnp.where(tril[None], a_w_s[j], 0.0)
            w_small = w_causal[:, :len_s, :len_s]
            wmix_s = jnp.einsum("gts,rq->gtrsq", w_small, eye_req).reshape(A_GROUPS, t_s, t_s)
            pad = A_CHUNK - t_s
            wmix_s = jnp.pad(wmix_s, ((0, 0), (0, pad), (0, pad)))
            bmix_p = a_b_s[j].T
            bmix_s = jnp.pad(jnp.repeat(a_b_s[j].T[:len_s], n_dec, axis=0), ((0, pad), (0, 0)))
            args = (a_w_in[j].astype(BF), row(a_ln_g[j]), row(a_ln_b[j]))
            w_out = a_w_out[j].astype(BF)
            hp, = _mixer_a(hp, g_mix, *args, w_causal.astype(BF), bmix_p, w_out, tm=1024,
                           emit_v=False)
            hs, v_s = _mixer_a(hs, g_mix, *args, wmix_s.astype(BF), bmix_s, w_out, tm=t_s,
                               emit_v=True)
            chunk_v_s.append(from_tm(v_s))
        elif kind == 1:
            nope_cols = (jnp.arange(MLA_HEADS)[:, None] * (MLA_NOPE + MLA_ROPE)
                         + jnp.arange(MLA_NOPE)[None, :]).reshape(-1)
            pe_cols = (jnp.arange(MLA_HEADS)[:, None] * (MLA_NOPE + MLA_ROPE) + MLA_NOPE
                       + jnp.arange(MLA_ROPE)[None, :])
            pe_cols_sw = _swap_halves(pe_cols)
            w = dict(
                wdq=b_w_dq[j].astype(BF), qn=row(b_q_norm[j]),
                wuqn=b_w_uq[j][:, nope_cols].astype(BF),
                wuqa=b_w_uq[j][:, pe_cols.reshape(-1)].astype(BF),
                wuqb=b_w_uq[j][:, pe_cols_sw.reshape(-1)].astype(BF),
                wdkv=b_w_dkv[j].astype(BF),
                wdkvb=_swap_halves(b_w_dkv[j][:, MLA_KV_LORA:]).astype(BF),
                kvn=row(b_kv_norm[j]),
                wuk=jnp.transpose(b_w_uk[j], (1, 2, 0)).astype(BF),
            )
            wuv = jnp.transpose(b_w_uv[j], (1, 0, 2)).astype(BF)
            wout = b_w_out[j].astype(BF)

            tq = 512
            kv_p, qt_p, vt_p = _mla_proj(hp, g_mix, cos_p, sin_p, w, tq=tq, feature_major=True)
            wuvt = jnp.transpose(b_w_uv[j], (1, 2, 0)).astype(BF)
            hp = _mla_prompt_attn(qt_p, kv_p, vt_p, hp, wuvt, b_w_out[j].T.astype(BF),
                                  n_batch=n_pr, seq=len_p, tq=tq)
            mla_p.append(kv_p.reshape(n_pr, len_p, MLA_LAT))

            kv_s, qlat_s, qpe_s = _mla_proj(hs, g_mix, cos_s, sin_s, w, tq=t_s,
                                            feature_major=False)

            def per_request(q):
                q = q.reshape(MLA_HEADS, len_s, n_dec, q.shape[-1])
                return jnp.transpose(q, (2, 0, 1, 3)).reshape(n_dec, MLA_HEADS * len_s, q.shape[-1])

            kv_new = jnp.pad(from_tm(kv_s), ((0, 0), (0, 16 - len_s), (0, 0)))
            o_s = _mla_decode_attn(page_table, per_request(qlat_s), per_request(qpe_s), kv_new,
                                   jnp.swapaxes(cache_mla, 2, 3), layer=j, n_new=len_s,
                                   pages_per_step=64)
            o_s = o_s.reshape(n_dec, MLA_HEADS, len_s, MLA_KV_LORA)
            o_s = jnp.transpose(o_s, (1, 2, 0, 3)).reshape(1, MLA_HEADS * t_s, MLA_KV_LORA)
            hs = _mla_out(hs, o_s, wuv, wout, tq=t_s)
            mla_s.append(from_tm(kv_s))
        else:
            w = dict(
                wx=c_w_x[j].astype(BF), wgate=c_w_gate[j].astype(BF), convw=c_conv_w[j],
                convb=row(c_conv_b[j]), wa=c_w_a[j].astype(BF), ba=row(c_b_a[j]),
                wi=c_w_i[j].astype(BF), bi=row(c_b_i[j]), lam=row(c_lambda[j]),
                wout=c_w_out[j].astype(BF),
            )
            hp, hl_p, cb_p = _mixer_c_prompt(hp, g_mix, w, n_batch=n_pr, seq=len_p, tt=256)
            lru_h_p.append(hl_p.reshape(n_pr, LRU_WIDTH))
            lru_c_p.append(cb_p)
            buf0 = jnp.swapaxes(state_lru_conv[j], 0, 1).reshape(-1, LRU_WIDTH)
            hs, hl_s, cb_s = _mixer_c_sample(hs, g_mix, state_lru_h[j], buf0, w, n_req=n_dec)
            lru_h_s.append(hl_s)
            lru_c_s.append(jnp.swapaxes(cb_s.reshape(CONV_WIDTH - 1, n_dec, LRU_WIDTH), 0, 1))

        last = layer == depth - 1
        g_ffn, g_out = row(norm_ffn[layer]), row(norm_out)
        hs, wg16, wu16, wd16 = _ffn_stream(hs, g_ffn, f_w_gate, f_w_up, f_w_down, g_out,
                                           layer=layer, tf=256, final_norm=last)
        hp = _ffn(hp, g_ffn, wg16, wu16, wd16, g_out, tm=1024, n_split=11, final_norm=last)

    return (hp.reshape(n_pr, len_p, D_MODEL), from_tm(hs), jnp.stack(chunk_v_s),
            jnp.stack(mla_p), jnp.stack(mla_s), jnp.stack(lru_h_p), jnp.stack(lru_h_s),
            jnp.stack(lru_c_p), jnp.stack(lru_c_s))
```

```python
import functools

import jax
import jax.numpy as jnp
import numpy as np
from jax import lax
from jax.experimental import pallas as pl
from jax.experimental.pallas import tpu as pltpu

BF = jnp.bfloat16
F32 = jnp.float32

D_MODEL = 1024
D_FF = 2816
A_CHUNK = 128
A_GROUPS = 8
MLA_HEADS = 8
MLA_Q_LORA = 384
MLA_KV_LORA = 256
MLA_NOPE = 128
MLA_ROPE = 64
MLA_V = 128
MLA_LAT = MLA_KV_LORA + MLA_ROPE
MLA_SCALE = (MLA_NOPE + MLA_ROPE) ** -0.5
ROPE_THETA = 10000.0
PAGE_SIZE = 128
LRU_WIDTH = 1280
LRU_BLOCKS = 10
LRU_BLOCK_DIM = 128
CONV_WIDTH = 4
LRU_C = 8.0
RMS_EPS = 1e-6
LN_EPS = 1e-5

SUBLANES = 8
LANES = 128
LOG2E = 1.4426950408889634
VMEM_LIMIT = 56 * 1024 * 1024


def _params(*sem):
    return pltpu.CompilerParams(dimension_semantics=sem, vmem_limit_bytes=VMEM_LIMIT)


def _dot(a, b):
    return jnp.dot(a.astype(BF), b.astype(BF), preferred_element_type=F32)


def _dot_nt(a, b):
    return lax.dot_general(a.astype(BF), b.astype(BF), (((1,), (1,)), ((), ())),
                           preferred_element_type=F32)


def _rmsnorm(x, g):
    return x * lax.rsqrt(jnp.mean(x * x, axis=-1, keepdims=True) + RMS_EPS) * g


def _layernorm(x, g, b):
    mu = jnp.mean(x, axis=-1, keepdims=True)
    xc = x - mu
    return xc * lax.rsqrt(jnp.mean(xc * xc, axis=-1, keepdims=True) + LN_EPS) * g + b


def _sigmoid(x):
    return 0.5 * jnp.tanh(0.5 * x) + 0.5


def _full(shape):
    return pl.BlockSpec(shape, lambda *_: (0,) * len(shape))


def _ffn_kernel(x_ref, g_ref, wg_ref, wu_ref, wd_ref, gout_ref, o_ref, h_scr, *,
                final_norm, n_split):
    x = x_ref[...]
    xn = _rmsnorm(x, g_ref[...]).astype(BF)
    width = D_FF // n_split
    for c in range(n_split):
        cols = slice(c * width, (c + 1) * width)
        gate = _dot(xn, wg_ref[:, cols])
        up = _dot(xn, wu_ref[:, cols])
        h_scr[:, cols] = (gate * _sigmoid(gate) * up).astype(BF)
    y = x + _dot(h_scr[...], wd_ref[...])
    if final_norm:
        y = _rmsnorm(y, gout_ref[...])
    o_ref[...] = y


def _ffn(h, g, wg, wu, wd, gout, *, tm, n_split, final_norm):
    t = h.shape[0]
    return pl.pallas_call(
        functools.partial(_ffn_kernel, final_norm=final_norm, n_split=n_split),
        grid=(t // tm,),
        in_specs=[
            pl.BlockSpec((tm, D_MODEL), lambda i: (i, 0)),
            _full((1, D_MODEL)),
            _full((D_MODEL, D_FF)),
            _full((D_MODEL, D_FF)),
            _full((D_FF, D_MODEL)),
            _full((1, D_MODEL)),
        ],
        out_specs=pl.BlockSpec((tm, D_MODEL), lambda i: (i, 0)),
        out_shape=jax.ShapeDtypeStruct((t, D_MODEL), F32),
        scratch_shapes=[pltpu.VMEM((tm, D_FF), BF)],
        compiler_params=_params("parallel"),
        name="ffn",
    )(h, g, wg, wu, wd, gout)


def _ffn_stream_kernel(x_ref, g_ref, wg_ref, wu_ref, wd_ref, gout_ref,
                       o_ref, wg16_ref, wu16_ref, wd16_ref, xn_scr, acc_scr, *, final_norm):
    j = pl.program_id(0)

    @pl.when(j == 0)
    def _():
        xn_scr[...] = _rmsnorm(x_ref[...], g_ref[...]).astype(BF)
        acc_scr[...] = jnp.zeros_like(acc_scr)

    wg, wu, wd = wg_ref[...].astype(BF), wu_ref[...].astype(BF), wd_ref[...].astype(BF)
    wg16_ref[...] = wg
    wu16_ref[...] = wu
    wd16_ref[...] = wd
    xn = xn_scr[...]
    gate = _dot(xn, wg)
    acc_scr[...] += _dot(gate * _sigmoid(gate) * _dot(xn, wu), wd)

    @pl.when(j == pl.num_programs(0) - 1)
    def _():
        y = x_ref[...] + acc_scr[...]
        if final_norm:
            y = _rmsnorm(y, gout_ref[...])
        o_ref[...] = y


def _ffn_stream(h, g, wg, wu, wd, gout, *, layer, tf, final_norm):
    t = h.shape[0]
    return pl.pallas_call(
        functools.partial(_ffn_stream_kernel, final_norm=final_norm),
        grid=(D_FF // tf,),
        in_specs=[
            _full((t, D_MODEL)),
            _full((1, D_MODEL)),
            pl.BlockSpec((None, D_MODEL, tf), lambda j: (layer, 0, j)),
            pl.BlockSpec((None, D_MODEL, tf), lambda j: (layer, 0, j)),
            pl.BlockSpec((None, tf, D_MODEL), lambda j: (layer, j, 0)),
            _full((1, D_MODEL)),
        ],
        out_specs=[_full((t, D_MODEL)),
                   pl.BlockSpec((D_MODEL, tf), lambda j: (0, j)),
                   pl.BlockSpec((D_MODEL, tf), lambda j: (0, j)),
                   pl.BlockSpec((tf, D_MODEL), lambda j: (j, 0))],
        out_shape=[jax.ShapeDtypeStruct((t, D_MODEL), F32),
                   jax.ShapeDtypeStruct((D_MODEL, D_FF), BF),
                   jax.ShapeDtypeStruct((D_MODEL, D_FF), BF),
                   jax.ShapeDtypeStruct((D_FF, D_MODEL), BF)],
        scratch_shapes=[pltpu.VMEM((t, D_MODEL), BF), pltpu.VMEM((t, D_MODEL), F32)],
        compiler_params=_params("arbitrary"),
        name="ffn_stream",
    )(h, g, wg, wu, wd, gout)


def _mixer_a_kernel(x_ref, g_ref, win_ref, lng_ref, lnb_ref, wmix_ref, bmix_ref, wout_ref,
                    o_ref, *rest, emit_v, n_sub):
    u_scr, vb_scr, uv_scr = rest[-3:]
    sub = x_ref.shape[0] // n_sub

    def tile(part):
        return slice(part * sub, (part + 1) * sub)

    def project_in(part):
        xn = _rmsnorm(x_ref[tile(part), :], g_ref[...]).astype(BF)
        u_scr[tile(part), :] = jax.nn.gelu(_dot(xn, win_ref[:, :D_MODEL]))
        v = _layernorm(jax.nn.gelu(_dot(xn, win_ref[:, D_MODEL:])), lng_ref[...], lnb_ref[...])
        if emit_v:
            rest[0][tile(part), :] = v
        vb_scr[tile(part), :] = v.astype(BF)

    def mix(part):
        for c in range(part * sub // A_CHUNK, (part + 1) * sub // A_CHUNK):
            rows = slice(c * A_CHUNK, (c + 1) * A_CHUNK)
            for g in range(A_GROUPS):
                cols = slice(g * A_CHUNK, (g + 1) * A_CHUNK)
                mixed = _dot(wmix_ref[g], vb_scr[rows, cols]) + bmix_ref[:, g:g + 1]
                uv_scr[rows, cols] = (u_scr[rows, cols] * mixed).astype(BF)

    def project_out(part):
        o_ref[tile(part), :] = x_ref[tile(part), :] + _dot(uv_scr[tile(part), :], wout_ref[...])

    for t in range(n_sub + 2):
        if t < n_sub:
            project_in(t)
        if 1 <= t <= n_sub:
            mix(t - 1)
        if t >= 2:
            project_out(t - 2)


def _mixer_a(h, g, win, lng, lnb, wmix, bmix, wout, *, tm, emit_v):
    t = h.shape[0]
    n_out = 2 if emit_v else 1
    return pl.pallas_call(
        functools.partial(_mixer_a_kernel, emit_v=emit_v, n_sub=max(1, tm // (2 * A_CHUNK))),
        grid=(t // tm,),
        in_specs=[
            pl.BlockSpec((tm, D_MODEL), lambda i: (i, 0)),
            _full((1, D_MODEL)),
            _full((D_MODEL, 2 * D_MODEL)),
            _full((1, D_MODEL)),
            _full((1, D_MODEL)),
            _full((A_GROUPS, A_CHUNK, A_CHUNK)),
            _full((A_CHUNK, A_GROUPS)),
            _full((D_MODEL, D_MODEL)),
        ],
        out_specs=[pl.BlockSpec((tm, D_MODEL), lambda i: (i, 0))] * n_out,
        out_shape=[jax.ShapeDtypeStruct((t, D_MODEL), F32)] * n_out,
        scratch_shapes=[pltpu.VMEM((tm, D_MODEL), F32), pltpu.VMEM((tm, D_MODEL), BF),
                        pltpu.VMEM((tm, D_MODEL), BF)],
        compiler_params=_params("parallel"),
        name="mixer_a",
    )(h, g, win, lng, lnb, wmix, bmix, wout)


def _mla_proj_kernel(x_ref, g_ref, cos_ref, sin_ref, wdq_ref, qn_ref, wuqn_ref, wuqa_ref,
                     wuqb_ref, wdkv_ref, wdkvb_ref, kvn_ref, wuk_ref,
                     kv_ref, qa_ref, qb_ref, *, feature_major):
    tq = x_ref.shape[0]
    xn = _rmsnorm(x_ref[...], g_ref[...]).astype(BF)
    cq = _rmsnorm(_dot(xn, wdq_ref[...]), qn_ref[...]).astype(BF)
    q_nope = _dot(cq, wuqn_ref[...])
    cos = cos_ref[...]
    sin = sin_ref[...]
    q_pe = _dot(cq, wuqa_ref[...]) * cos + _dot(cq, wuqb_ref[...]) * sin
    kv = _dot(xn, wdkv_ref[...])
    c_kv = _rmsnorm(kv[:, :MLA_KV_LORA], kvn_ref[...])
    kv_ref[:, :MLA_KV_LORA] = c_kv
    kv_ref[:, MLA_KV_LORA:] = (kv[:, MLA_KV_LORA:] * cos[:, :MLA_ROPE]
                               + _dot(xn, wdkvb_ref[...]) * sin[:, :MLA_ROPE])
    if feature_major:
        qb_ref[0] = c_kv.T.astype(BF)
        q_pe_t = q_pe.T
    for h in range(MLA_HEADS):
        rows = slice(h * tq, (h + 1) * tq)
        q_lat = _dot(q_nope[:, h * MLA_NOPE:(h + 1) * MLA_NOPE], wuk_ref[h])
        if feature_major:
            qa_ref[0, :MLA_KV_LORA, rows] = q_lat.T.astype(BF)
            qa_ref[0, MLA_KV_LORA:, rows] = q_pe_t[h * MLA_ROPE:(h + 1) * MLA_ROPE, :].astype(BF)
        else:
            qa_ref[0, rows, :] = q_lat.astype(BF)
            qb_ref[0, rows, :] = q_pe[:, h * MLA_ROPE:(h + 1) * MLA_ROPE].astype(BF)


def _mla_proj(h, g, cos, sin, w, *, tq, feature_major):
    t = h.shape[0]
    nb = t // tq
    npos = cos.shape[0] // tq
    hq = MLA_HEADS * tq
    qa_shape, qb_shape = (((MLA_LAT, hq), (MLA_KV_LORA, tq)) if feature_major
                          else ((hq, MLA_KV_LORA), (hq, MLA_ROPE)))
    return pl.pallas_call(
        functools.partial(_mla_proj_kernel, feature_major=feature_major),
        grid=(nb,),
        in_specs=[
            pl.BlockSpec((tq, D_MODEL), lambda i: (i, 0)),
            _full((1, D_MODEL)),
            pl.BlockSpec((tq, MLA_HEADS * MLA_ROPE), lambda i: (i % npos, 0)),
            pl.BlockSpec((tq, MLA_HEADS * MLA_ROPE), lambda i: (i % npos, 0)),
            _full((D_MODEL, MLA_Q_LORA)),
            _full((1, MLA_Q_LORA)),
            _full((MLA_Q_LORA, MLA_HEADS * MLA_NOPE)),
            _full((MLA_Q_LORA, MLA_HEADS * MLA_ROPE)),
            _full((MLA_Q_LORA, MLA_HEADS * MLA_ROPE)),
            _full((D_MODEL, MLA_LAT)),
            _full((D_MODEL, MLA_ROPE)),
            _full((1, MLA_KV_LORA)),
            _full((MLA_HEADS, MLA_NOPE, MLA_KV_LORA)),
        ],
        out_specs=[pl.BlockSpec((tq, MLA_LAT), lambda i: (i, 0)),
                   pl.BlockSpec((1,) + qa_shape, lambda i: (i, 0, 0)),
                   pl.BlockSpec((1,) + qb_shape, lambda i: (i, 0, 0))],
        out_shape=[jax.ShapeDtypeStruct((t, MLA_LAT), F32),
                   jax.ShapeDtypeStruct((nb,) + qa_shape, BF),
                   jax.ShapeDtypeStruct((nb,) + qb_shape, BF)],
        compiler_params=_params("parallel"),
        name="mla_proj",
    )(h, g, cos, sin, w["wdq"], w["qn"], w["wuqn"], w["wuqa"], w["wuqb"], w["wdkv"],
      w["wdkvb"], w["kvn"], w["wuk"])


def _lanes(x, n):
    return x[:, :n] if n <= LANES else jnp.concatenate([x] * (n // LANES), axis=-1)


def _softmax_step(s, pv, m_scr, l_scr, acc_scr, rows):
    c2 = MLA_SCALE * LOG2E
    m_prev = m_scr[rows, :]
    m_new = jnp.maximum(m_prev, jnp.max(s, axis=-1, keepdims=True))
    alpha = jnp.exp2((m_prev - m_new) * c2)
    p = jnp.exp2((s - _lanes(m_new, s.shape[-1])) * c2)
    l_scr[rows, :] = alpha * l_scr[rows, :] + jnp.sum(p, axis=-1, keepdims=True)
    acc_scr[rows, :] = _lanes(alpha, MLA_KV_LORA) * acc_scr[rows, :] + pv(p)
    m_scr[rows, :] = m_new


def _mla_value_out(o_head, x, wuv_ref, wout_ref, ov_scr):
    for h in range(MLA_HEADS):
        ov_scr[:, h * MLA_V:(h + 1) * MLA_V] = _dot(o_head(h), wuv_ref[h]).astype(BF)
    return x + _dot(ov_scr[...], wout_ref[...])


def _mla_prompt_attn_kernel(qi_ref, kj_ref, qt_ref, kv_ref, vt_ref, x_ref, wuvt_ref, woutt_ref,
                            y_ref, m_scr, l_scr, acc_scr, ov_scr, *, tq, tk, cb):
    step_id = pl.program_id(1)
    i = qi_ref[step_id]
    j = kj_ref[step_id]
    j_diag = ((i + 1) * tq - 1) // tk
    c2 = MLA_SCALE * LOG2E

    @pl.when(j == 0)
    def _():
        m_scr[...] = jnp.full_like(m_scr, -jnp.inf)
        l_scr[...] = jnp.zeros_like(l_scr)
        acc_scr[...] = jnp.zeros_like(acc_scr)

    def step(masked):
        kb = kv_ref[...].astype(BF)
        klat, kpe = kb[:, :MLA_KV_LORA], kb[:, MLA_KV_LORA:]
        vt = vt_ref[0]
        n_blocks = MLA_HEADS * tq // cb

        def scores(c):
            cols = slice(c * cb, (c + 1) * cb)
            return (_dot(klat, qt_ref[0, :MLA_KV_LORA, cols])
                    + _dot(kpe, qt_ref[0, MLA_KV_LORA:, cols]))

        def softmax(c, s):
            cols = slice(c * cb, (c + 1) * cb)
            if masked:
                k_pos = j * tk + lax.broadcasted_iota(jnp.int32, (tk, cb), 0)
                tok = (c * cb) % tq + lax.broadcasted_iota(jnp.int32, (tk, cb), 1)
                s = jnp.where(k_pos <= i * tq + tok, s, -jnp.inf)
            m_prev = m_scr[:, cols]
            m_new = jnp.maximum(m_prev, jnp.max(s, axis=0, keepdims=True))
            alpha = jnp.exp2((m_prev - m_new) * c2)
            p = jnp.exp2((s - m_new) * c2)
            l_scr[:, cols] = alpha * l_scr[:, cols] + jnp.sum(p, axis=0, keepdims=True)
            m_scr[:, cols] = m_new
            return p.astype(BF), alpha

        def values(c, p, alpha):
            cols = slice(c * cb, (c + 1) * cb)
            acc_scr[:, cols] = alpha * acc_scr[:, cols] + _dot(vt, p)

        s_tiles, p_tiles = {}, {}
        for t in range(n_blocks + 2):
            if t < n_blocks:
                s_tiles[t] = scores(t)
            if 1 <= t <= n_blocks:
                p_tiles[t - 1] = softmax(t - 1, s_tiles.pop(t - 1))
            if t >= 2:
                values(t - 2, *p_tiles.pop(t - 2))

    @pl.when(j < j_diag)
    def _():
        step(False)

    @pl.when(j == j_diag)
    def _():
        step(True)
        for h in range(MLA_HEADS):
            cols = slice(h * tq, (h + 1) * tq)
            o_t = acc_scr[:, cols] / l_scr[:, cols]
            ov_scr[h * MLA_V:(h + 1) * MLA_V, :] = _dot(wuvt_ref[h], o_t).astype(BF)
        y_ref[...] = x_ref[...] + _dot(woutt_ref[...], ov_scr[...]).T


def _causal_schedule(nq, tq, tk):
    pairs = [(i, j) for i in range(nq) for j in range(((i + 1) * tq - 1) // tk + 1)]
    qi, kj = zip(*pairs)
    return np.asarray(qi, np.int32), np.asarray(kj, np.int32)


def _mla_prompt_attn(qt, kv, vt, h, wuvt, woutt, *, n_batch, seq, tq):
    tk = tq
    nq = seq // tq
    hq = MLA_HEADS * tq
    qi, kj = _causal_schedule(nq, tq, tk)

    def q_map(b, s, qi_ref, kj_ref):
        return (b * nq + qi_ref[s], 0, 0)

    def k_map(b, s, qi_ref, kj_ref):
        return (b * nq + kj_ref[s], 0)

    def v_map(b, s, qi_ref, kj_ref):
        return (b * nq + kj_ref[s], 0, 0)

    def tok_map(b, s, qi_ref, kj_ref):
        return (b * nq + qi_ref[s], 0)

    grid_spec = pltpu.PrefetchScalarGridSpec(
        num_scalar_prefetch=2,
        grid=(n_batch, len(qi)),
        in_specs=[
            pl.BlockSpec((1, MLA_LAT, hq), q_map),
            pl.BlockSpec((tk, MLA_LAT), k_map),
            pl.BlockSpec((1, MLA_KV_LORA, tk), v_map),
            pl.BlockSpec((tq, D_MODEL), tok_map),
            _full((MLA_HEADS, MLA_V, MLA_KV_LORA)),
            _full((D_MODEL, MLA_HEADS * MLA_V)),
        ],
        out_specs=pl.BlockSpec((tq, D_MODEL), tok_map),
        scratch_shapes=[pltpu.VMEM((1, hq), F32), pltpu.VMEM((1, hq), F32),
                        pltpu.VMEM((MLA_KV_LORA, hq), F32),
                        pltpu.VMEM((MLA_HEADS * MLA_V, tq), BF)],
    )
    return pl.pallas_call(
        functools.partial(_mla_prompt_attn_kernel, tq=tq, tk=tk, cb=256),
        grid_spec=grid_spec,
        out_shape=jax.ShapeDtypeStruct((n_batch * seq, D_MODEL), F32),
        compiler_params=_params("parallel", "arbitrary"),
        name="mla_prompt_attn",
    )(qi, kj, qt, kv, vt, h, wuvt, woutt)


def _mla_decode_attn_kernel(pt_ref, qlat_ref, qpe_ref, kvnew_ref, *rest, n_pages, n_new, group):
    del pt_ref
    page_refs = rest[:n_pages]
    o_ref, kbuf, m_scr, l_scr, acc_scr = rest[n_pages:]
    s_id = pl.program_id(1)
    qlat = qlat_ref[0]
    qpe = qpe_ref[0]
    every_row = slice(None)

    @pl.when(s_id == 0)
    def _():
        m_scr[...] = jnp.full_like(m_scr, -jnp.inf)
        l_scr[...] = jnp.zeros_like(l_scr)
        acc_scr[...] = jnp.zeros_like(acc_scr)
        kn = kvnew_ref[0].astype(BF)
        klat, kpe = kn[:, :MLA_KV_LORA], kn[:, MLA_KV_LORA:]
        s = _dot_nt(qlat, klat) + _dot_nt(qpe, kpe)
        tok = lax.broadcasted_iota(jnp.int32, s.shape, 0) & (n_new - 1)
        col = lax.broadcasted_iota(jnp.int32, s.shape, 1)
        s = jnp.where(col <= tok, s, -jnp.inf)
        _softmax_step(s, lambda p: _dot(p, klat), m_scr, l_scr, acc_scr, every_row)

    n_groups = n_pages // group
    width = group * PAGE_SIZE

    def scores(g):
        for p_id in range(g * group, (g + 1) * group):
            kbuf[:, p_id * PAGE_SIZE:(p_id + 1) * PAGE_SIZE] = page_refs[p_id][...].astype(BF)
        keys = slice(g * width, (g + 1) * width)
        return _dot(qlat, kbuf[:MLA_KV_LORA, keys]) + _dot(qpe, kbuf[MLA_KV_LORA:, keys])

    s_next = scores(0)
    for g in range(n_groups):
        s = s_next
        if g + 1 < n_groups:
            s_next = scores(g + 1)
        keys = slice(g * width, (g + 1) * width)
        _softmax_step(s, lambda p: _dot_nt(p, kbuf[:MLA_KV_LORA, keys]), m_scr, l_scr, acc_scr,
                      every_row)

    @pl.when(s_id == pl.num_programs(1) - 1)
    def _():
        o_ref[0] = acc_scr[...] / _lanes(l_scr[...], MLA_KV_LORA)


def _mla_decode_attn(page_table, qlat, qpe, kvnew, cache_t, *, layer, n_new, pages_per_step):
    n_req, n_pages_total = page_table.shape
    rows = qlat.shape[1]
    steps = n_pages_total // pages_per_step

    def page_map(p_id):
        return lambda r, s, pt: (layer, pt[r, s * pages_per_step + p_id], 0, 0)

    grid_spec = pltpu.PrefetchScalarGridSpec(
        num_scalar_prefetch=1,
        grid=(n_req, steps),
        in_specs=[
            pl.BlockSpec((1, rows, MLA_KV_LORA), lambda r, s, pt: (r, 0, 0)),
            pl.BlockSpec((1, rows, MLA_ROPE), lambda r, s, pt: (r, 0, 0)),
            pl.BlockSpec((1,) + kvnew.shape[1:], lambda r, s, pt: (r, 0, 0)),
        ] + [pl.BlockSpec((None, None, MLA_LAT, PAGE_SIZE), page_map(p_id))
             for p_id in range(pages_per_step)],
        out_specs=pl.BlockSpec((1, rows, MLA_KV_LORA), lambda r, s, pt: (r, 0, 0)),
        scratch_shapes=[pltpu.VMEM((MLA_LAT, pages_per_step * PAGE_SIZE), BF),
                        pltpu.VMEM((rows, LANES), F32), pltpu.VMEM((rows, LANES), F32),
                        pltpu.VMEM((rows, MLA_KV_LORA), F32)],
    )
    return pl.pallas_call(
        functools.partial(_mla_decode_attn_kernel, n_pages=pages_per_step, n_new=n_new,
                          group=16),
        grid_spec=grid_spec,
        out_shape=jax.ShapeDtypeStruct((n_req, rows, MLA_KV_LORA), F32),
        compiler_params=_params("parallel", "arbitrary"),
        name="mla_decode_attn",
    )(page_table, qlat, qpe, kvnew, *([cache_t] * pages_per_step))


def _mla_out_kernel(x_ref, o_ref, wuv_ref, wout_ref, y_ref, ov_scr):
    tq = x_ref.shape[0]
    y_ref[...] = _mla_value_out(lambda h: o_ref[0, h * tq:(h + 1) * tq, :], x_ref[...],
                                wuv_ref, wout_ref, ov_scr)


def _mla_out(h, o, wuv, wout, *, tq):
    t = h.shape[0]
    hq = MLA_HEADS * tq
    return pl.pallas_call(
        _mla_out_kernel,
        grid=(t // tq,),
        in_specs=[
            pl.BlockSpec((tq, D_MODEL), lambda i: (i, 0)),
            pl.BlockSpec((1, hq, MLA_KV_LORA), lambda i: (i, 0, 0)),
            _full((MLA_HEADS, MLA_KV_LORA, MLA_V)),
            _full((MLA_HEADS * MLA_V, D_MODEL)),
        ],
        out_specs=pl.BlockSpec((tq, D_MODEL), lambda i: (i, 0)),
        out_shape=jax.ShapeDtypeStruct((t, D_MODEL), F32),
        scratch_shapes=[pltpu.VMEM((tq, MLA_HEADS * MLA_V), BF)],
        compiler_params=_params("parallel"),
        name="mla_out",
    )(h, o, wuv, wout)


def _lru_coeffs(xc, wgates_ref, bgates_ref, lam_ref):
    neg_lam = -lam_ref[...]
    softplus = jnp.maximum(neg_lam, 0.0) + jnp.log1p(jnp.exp(-jnp.abs(neg_lam)))
    k = (-0.5 * LRU_C * LOG2E) * softplus
    a_blocks, b_blocks = [], []
    for n in range(LRU_BLOCKS):
        cols = slice(n * LRU_BLOCK_DIM, (n + 1) * LRU_BLOCK_DIM)
        x_n = xc[:, cols]
        t = jnp.tanh(_dot(x_n, wgates_ref[n]) + bgates_ref[n])
        a = jnp.exp2(t[:, :LRU_BLOCK_DIM] * k[:, cols] + k[:, cols])
        u = 1.0 - a * a
        root = jnp.where(u > 0.0, u * lax.rsqrt(u), 0.0)
        a_blocks.append(a)
        b_blocks.append(root * (0.5 * t[:, LRU_BLOCK_DIM:] + 0.5) * x_n)
    return a_blocks, b_blocks


def _scan_pitch(seg):
    pitch = seg + SUBLANES
    return pitch if (pitch // SUBLANES) % 2 else pitch + SUBLANES


def _mixer_c_prompt_kernel(x_ref, g_ref, wx_ref, wgate_ref, convw_ref, convb_ref, wgates_ref,
                           bgates_ref, lam_ref, wout_ref,
                           o_ref, hlast_ref, buf_ref,
                           xx_scr, gate_scr, a_scr, b_scr, hloc_scr, ploc_scr, carry_scr):
    tt = x_ref.shape[0]
    seg = tt // SUBLANES
    pitch = _scan_pitch(seg)
    ti = pl.program_id(1)

    @pl.when(ti == 0)
    def _():
        xx_scr[0:SUBLANES, :] = jnp.zeros((SUBLANES, LRU_WIDTH), F32)
        carry_scr[...] = jnp.zeros_like(carry_scr)

    x = x_ref[...]
    xn = _rmsnorm(x, g_ref[...]).astype(BF)
    gate_scr[...] = jax.nn.gelu(_dot(xn, wgate_ref[...]))
    xx_scr[SUBLANES:, :] = _dot(xn, wx_ref[...])
    xx = xx_scr[...]
    xc = convb_ref[...] + convw_ref[CONV_WIDTH - 1:CONV_WIDTH, :] * xx[SUBLANES:, :]
    for d in range(1, CONV_WIDTH):
        k = CONV_WIDTH - 1 - d
        xc = xc + convw_ref[k:k + 1, :] * pltpu.roll(xx, shift=d, axis=0)[SUBLANES:, :]
    buf_ref[0] = xx_scr[tt + SUBLANES - (CONV_WIDTH - 1):, :]
    xx_scr[0:SUBLANES, :] = xx_scr[tt:tt + SUBLANES, :]

    a, b = _lru_coeffs(xc, wgates_ref, bgates_ref, lam_ref)
    for n in range(LRU_BLOCKS):
        for k in range(SUBLANES):
            a_scr[n, k * pitch:k * pitch + seg, :] = a[n][k * seg:(k + 1) * seg, :]
            b_scr[n, k * pitch:k * pitch + seg, :] = b[n][k * seg:(k + 1) * seg, :]

    def body(j, carry):
        hs, ps = carry
        rows = pl.ds(j, SUBLANES, stride=pitch)
        new_h, new_p = [], []
        for n in range(LRU_BLOCKS):
            aj = a_scr[n, rows, :]
            h = aj * hs[n] + b_scr[n, rows, :]
            p = aj * ps[n]
            hloc_scr[n, rows, :] = h
            ploc_scr[n, rows, :] = p
            new_h.append(h)
            new_p.append(p)
        return tuple(new_h), tuple(new_p)

    h_end, p_end = lax.fori_loop(
        0, seg, body,
        ((jnp.zeros((SUBLANES, LRU_BLOCK_DIM), F32),) * LRU_BLOCKS,
         (jnp.ones((SUBLANES, LRU_BLOCK_DIM), F32),) * LRU_BLOCKS))

    for n in range(LRU_BLOCKS):
        cols = slice(n * LRU_BLOCK_DIM, (n + 1) * LRU_BLOCK_DIM)
        c = carry_scr[:, cols]
        for k in range(SUBLANES):
            rows = slice(k * seg, (k + 1) * seg)
            srows = slice(k * pitch, k * pitch + seg)
            gate_scr[rows, cols] = gate_scr[rows, cols] * (hloc_scr[n, srows, :]
                                                           + ploc_scr[n, srows, :] * c)
            c = p_end[n][k:k + 1, :] * c + h_end[n][k:k + 1, :]
        carry_scr[:, cols] = c
    hlast_ref[0] = carry_scr[...]
    o_ref[...] = x + _dot(gate_scr[...], wout_ref[...])


def _mixer_c_prompt(h, g, w, *, n_batch, seq, tt):
    nt = seq // tt
    wd = LRU_WIDTH
    blk = (LRU_BLOCKS, LRU_BLOCK_DIM, 2 * LRU_BLOCK_DIM)
    return pl.pallas_call(
        _mixer_c_prompt_kernel,
        grid=(n_batch, nt),
        in_specs=[
            pl.BlockSpec((tt, D_MODEL), lambda b, i: (b * nt + i, 0)),
            _full((1, D_MODEL)),
            _full((D_MODEL, wd)), _full((D_MODEL, wd)),
            _full((CONV_WIDTH, wd)), _full((1, wd)),
            _full(blk), _full((LRU_BLOCKS, 1, 2 * LRU_BLOCK_DIM)), _full((1, wd)),
            _full((wd, D_MODEL)),
        ],
        out_specs=[pl.BlockSpec((tt, D_MODEL), lambda b, i: (b * nt + i, 0)),
                   pl.BlockSpec((1, 1, wd), lambda b, i: (b, 0, 0)),
                   pl.BlockSpec((1, CONV_WIDTH - 1, wd), lambda b, i: (b, 0, 0))],
        out_shape=[jax.ShapeDtypeStruct((n_batch * seq, D_MODEL), F32),
                   jax.ShapeDtypeStruct((n_batch, 1, wd), F32),
                   jax.ShapeDtypeStruct((n_batch, CONV_WIDTH - 1, wd), F32)],
        scratch_shapes=[pltpu.VMEM((tt + SUBLANES, wd), F32), pltpu.VMEM((tt, wd), F32)] +
                       [pltpu.VMEM((LRU_BLOCKS, SUBLANES * _scan_pitch(tt // SUBLANES),
                                    LRU_BLOCK_DIM), F32)] * 4 +
                       [pltpu.VMEM((1, wd), F32)],
        compiler_params=_params("parallel", "arbitrary"),
        name="mixer_c_prompt",
    )(h, g, w["wx"], w["wgate"], w["convw"], w["convb"], w["wgates"], w["bgates"], w["lam"],
      w["wout"])


def _mixer_c_sample_kernel(x_ref, g_ref, h0_ref, buf0_ref, wx_ref, wgate_ref, convw_ref, convb_ref,
                           wgates_ref, bgates_ref, lam_ref, wout_ref,
                           o_ref, hlast_ref, buf_ref, xx_scr, gh_scr, *, n_req):
    t = x_ref.shape[0]
    hist = (CONV_WIDTH - 1) * n_req
    x = x_ref[...]
    xn = _rmsnorm(x, g_ref[...]).astype(BF)
    gate = jax.nn.gelu(_dot(xn, wgate_ref[...]))
    xx_scr[0:hist, :] = buf0_ref[...]
    xx_scr[hist:, :] = _dot(xn, wx_ref[...])
    xc = convb_ref[...]
    for k in range(CONV_WIDTH):
        xc = xc + convw_ref[k:k + 1, :] * xx_scr[k * n_req:k * n_req + t, :]
    buf_ref[...] = xx_scr[t:, :]
    a, b = (jnp.concatenate(blocks, axis=-1)
            for blocks in _lru_coeffs(xc, wgates_ref, bgates_ref, lam_ref))
    h = h0_ref[...]
    for step in range(t // n_req):
        rows = slice(step * n_req, (step + 1) * n_req)
        h = a[rows, :] * h + b[rows, :]
        gh_scr[rows, :] = gate[rows, :] * h
    hlast_ref[...] = h
    o_ref[...] = x + _dot(gh_scr[...], wout_ref[...])


def _mixer_c_sample(h, g, h0, buf0, w, *, n_req):
    t = h.shape[0]
    wd = LRU_WIDTH
    hist = (CONV_WIDTH - 1) * n_req
    blk = (LRU_BLOCKS, LRU_BLOCK_DIM, 2 * LRU_BLOCK_DIM)
    return pl.pallas_call(
        functools.partial(_mixer_c_sample_kernel, n_req=n_req),
        grid=(1,),
        in_specs=[
            _full((t, D_MODEL)), _full((1, D_MODEL)), _full((n_req, wd)), _full((hist, wd)),
            _full((D_MODEL, wd)), _full((D_MODEL, wd)),
            _full((CONV_WIDTH, wd)), _full((1, wd)),
            _full(blk), _full((LRU_BLOCKS, 1, 2 * LRU_BLOCK_DIM)), _full((1, wd)),
            _full((wd, D_MODEL)),
        ],
        out_specs=[_full((t, D_MODEL)), _full((n_req, wd)), _full((hist, wd))],
        out_shape=[jax.ShapeDtypeStruct((t, D_MODEL), F32),
                   jax.ShapeDtypeStruct((n_req, wd), F32),
                   jax.ShapeDtypeStruct((hist, wd), F32)],
        scratch_shapes=[pltpu.VMEM((hist + t, wd), F32), pltpu.VMEM((t, wd), F32)],
        compiler_params=_params("arbitrary"),
        name="mixer_c_sample",
    )(h, g, h0, buf0, w["wx"], w["wgate"], w["convw"], w["convb"], w["wgates"], w["bgates"],
      w["lam"], w["wout"])


def _rope_tables(pos):
    inv = 1.0 / (ROPE_THETA ** (jnp.arange(0, MLA_ROPE, 2, dtype=F32) / MLA_ROPE))
    ang = pos.astype(F32)[:, None] * inv[None, :]
    cos, sin = jnp.cos(ang), jnp.sin(ang)
    cos_t = jnp.tile(jnp.concatenate([cos, cos], axis=-1), (1, MLA_HEADS))
    sin_t = jnp.tile(jnp.concatenate([-sin, sin], axis=-1), (1, MLA_HEADS))
    return cos_t, sin_t


def _swap_halves(w):
    half = w.shape[-1] // 2
    return jnp.concatenate([w[..., half:], w[..., :half]], axis=-1)


def kernel(x_prompt, x_sample, cache_mla, state_lru_h, state_lru_conv, page_table, norm_mix, norm_ffn, norm_out, a_w_in, a_ln_g, a_ln_b, a_w_s, a_b_s, a_w_out, b_w_dq, b_q_norm, b_w_uq, b_w_dkv, b_kv_norm, b_w_uk, b_w_uv, b_w_out, c_w_x, c_w_gate, c_conv_w, c_conv_b, c_w_a, c_b_a, c_w_i, c_b_i, c_lambda, c_w_out, f_w_gate, f_w_up, f_w_down):
    n_pr, len_p, _ = x_prompt.shape
    n_dec, len_s, _ = x_sample.shape
    depth = norm_mix.shape[0]
    past_len = page_table.shape[1] * PAGE_SIZE
    t_s = n_dec * len_s
    assert len_p % A_CHUNK == 0 and len_s <= A_CHUNK and (len_s & (len_s - 1)) == 0

    def row(v):
        return v.reshape(1, -1)

    def to_tm(v):
        return jnp.swapaxes(v, 0, 1).reshape((t_s,) + v.shape[2:])

    def from_tm(v):
        return jnp.swapaxes(v.reshape((len_s, n_dec) + v.shape[1:]), 0, 1)

    hp = x_prompt.reshape(n_pr * len_p, D_MODEL)
    hs = to_tm(x_sample)

    cos_p, sin_p = _rope_tables(jnp.arange(len_p, dtype=jnp.int32))
    pos_s = past_len + jnp.arange(len_s, dtype=jnp.int32)
    cos_s, sin_s = _rope_tables(jnp.repeat(pos_s, n_dec))

    tril = jnp.tril(jnp.ones((A_CHUNK, A_CHUNK), dtype=bool))
    eye_req = jnp.eye(n_dec, dtype=F32)

    chunk_v_s, mla_p, mla_s = [], [], []
    lru_h_p, lru_h_s, lru_c_p, lru_c_s = [], [], [], []
    for layer in range(depth):
        kind, j = layer % 3, layer // 3
        g_mix = row(norm_mix[layer])
        if kind == 0:
            w_causal = jnp.where(tril[None], a_w_s[j], 0.0)
            w_small = w_causal[:, :len_s, :len_s]
            wmix_s = jnp.einsum("gts,rq->gtrsq", w_small, eye_req).reshape(A_GROUPS, t_s, t_s)
            pad = A_CHUNK - t_s
            wmix_s = jnp.pad(wmix_s, ((0, 0), (0, pad), (0, pad)))
            bmix_p = a_b_s[j].T
            bmix_s = jnp.pad(jnp.repeat(a_b_s[j].T[:len_s], n_dec, axis=0), ((0, pad), (0, 0)))
            args = (a_w_in[j].astype(BF), row(a_ln_g[j]), row(a_ln_b[j]))
            w_out = a_w_out[j].astype(BF)
            hp, = _mixer_a(hp, g_mix, *args, w_causal.astype(BF), bmix_p, w_out, tm=1024,
                           emit_v=False)
            hs, v_s = _mixer_a(hs, g_mix, *args, wmix_s.astype(BF), bmix_s, w_out, tm=t_s,
                               emit_v=True)
            chunk_v_s.append(from_tm(v_s))
        elif kind == 1:
            nope_cols = (jnp.arange(MLA_HEADS)[:, None] * (MLA_NOPE + MLA_ROPE)
                         + jnp.arange(MLA_NOPE)[None, :]).reshape(-1)
            pe_cols = (jnp.arange(MLA_HEADS)[:, None] * (MLA_NOPE + MLA_ROPE) + MLA_NOPE
                       + jnp.arange(MLA_ROPE)[None, :])
            pe_cols_sw = _swap_halves(pe_cols)
            w = dict(
                wdq=b_w_dq[j].astype(BF), qn=row(b_q_norm[j]),
                wuqn=b_w_uq[j][:, nope_cols].astype(BF),
                wuqa=b_w_uq[j][:, pe_cols.reshape(-1)].astype(BF),
                wuqb=b_w_uq[j][:, pe_cols_sw.reshape(-1)].astype(BF),
                wdkv=b_w_dkv[j].astype(BF),
                wdkvb=_swap_halves(b_w_dkv[j][:, MLA_KV_LORA:]).astype(BF),
                kvn=row(b_kv_norm[j]),
                wuk=jnp.transpose(b_w_uk[j], (1, 2, 0)).astype(BF),
            )
            wuv = jnp.transpose(b_w_uv[j], (1, 0, 2)).astype(BF)
            wout = b_w_out[j].astype(BF)

            tq = 512
            kv_p, qt_p, vt_p = _mla_proj(hp, g_mix, cos_p, sin_p, w, tq=tq, feature_major=True)
            wuvt = jnp.transpose(b_w_uv[j], (1, 2, 0)).astype(BF)
            hp = _mla_prompt_attn(qt_p, kv_p, vt_p, hp, wuvt, b_w_out[j].T.astype(BF),
                                  n_batch=n_pr, seq=len_p, tq=tq)
            mla_p.append(kv_p.reshape(n_pr, len_p, MLA_LAT))

            kv_s, qlat_s, qpe_s = _mla_proj(hs, g_mix, cos_s, sin_s, w, tq=t_s,
                                            feature_major=False)

            def per_request(q):
                q = q.reshape(MLA_HEADS, len_s, n_dec, q.shape[-1])
                return jnp.transpose(q, (2, 0, 1, 3)).reshape(n_dec, MLA_HEADS * len_s, q.shape[-1])

            kv_new = jnp.pad(from_tm(kv_s), ((0, 0), (0, 16 - len_s), (0, 0)))
            o_s = _mla_decode_attn(page_table, per_request(qlat_s), per_request(qpe_s), kv_new,
                                   jnp.swapaxes(cache_mla, 2, 3), layer=j, n_new=len_s,
                                   pages_per_step=64)
            o_s = o_s.reshape(n_dec, MLA_HEADS, len_s, MLA_KV_LORA)
            o_s = jnp.transpose(o_s, (1, 2, 0, 3)).reshape(1, MLA_HEADS * t_s, MLA_KV_LORA)
            hs = _mla_out(hs, o_s, wuv, wout, tq=t_s)
            mla_s.append(from_tm(kv_s))
        else:
            def blocks(v):
                return v.reshape(LRU_BLOCKS, 1, LRU_BLOCK_DIM)

            w = dict(
                wx=c_w_x[j].astype(BF), wgate=c_w_gate[j].astype(BF), convw=c_conv_w[j],
                convb=row(c_conv_b[j]), lam=row(c_lambda[j]), wout=c_w_out[j].astype(BF),
                wgates=(0.5 * jnp.concatenate([c_w_a[j], c_w_i[j]], axis=-1)).astype(BF),
                bgates=0.5 * jnp.concatenate([blocks(c_b_a[j]), blocks(c_b_i[j])], axis=-1),
            )
            hp, hl_p, cb_p = _mixer_c_prompt(hp, g_mix, w, n_batch=n_pr, seq=len_p, tt=512)
            lru_h_p.append(hl_p.reshape(n_pr, LRU_WIDTH))
            lru_c_p.append(cb_p)
            buf0 = jnp.swapaxes(state_lru_conv[j], 0, 1).reshape(-1, LRU_WIDTH)
            hs, hl_s, cb_s = _mixer_c_sample(hs, g_mix, state_lru_h[j], buf0, w, n_req=n_dec)
            lru_h_s.append(hl_s)
            lru_c_s.append(jnp.swapaxes(cb_s.reshape(CONV_WIDTH - 1, n_dec, LRU_WIDTH), 0, 1))

        last = layer == depth - 1
        g_ffn, g_out = row(norm_ffn[layer]), row(norm_out)
        hs, wg16, wu16, wd16 = _ffn_stream(hs, g_ffn, f_w_gate, f_w_up, f_w_down, g_out,
                                           layer=layer, tf=256, final_norm=last)
        hp = _ffn(hp, g_ffn, wg16, wu16, wd16, g_out, tm=1024, n_split=11, final_norm=last)

    return (hp.reshape(n_pr, len_p, D_MODEL), from_tm(hs), jnp.stack(chunk_v_s),
            jnp.stack(mla_p), jnp.stack(mla_s), jnp.stack(lru_h_p), jnp.stack(lru_h_s),
            jnp.stack(lru_c_p), jnp.stack(lru_c_s))
```
